```python
import math
import jax, jax.numpy as jnp
from jax import lax
import numpy as np

D_MODEL = 1024
BATCH = 8
SEQ = 2048
DEPTH = 4
DEC_BATCH = 128
DEC_SEQ = 8
PAST_LEN = 16384
PAGE_SIZE = 128

D_MIX = 2 * D_MODEL
POOL_WIDTH = D_MIX // 4
POOL_WINDOWS = (2, 4, 8, 16)
N_POOL_GROUPS = 4
POOL_GROUP_DIM = POOL_WIDTH // N_POOL_GROUPS
POOL_STATE_LEN = 15
SSD_WIDTH = D_MIX // 2
SSD_HEAD_DIM = 64
SSD_HEADS = SSD_WIDTH // SSD_HEAD_DIM
SSD_GROUPS = 2
SSD_HEADS_PER_GROUP = SSD_HEADS // SSD_GROUPS
D_STATE = 128
CONV_WIDTH = 4
CONV_DIM = SSD_WIDTH + 2 * SSD_GROUPS * D_STATE
CHUNK = 128
XA_WIDTH = D_MIX // 4
XA_HEADS = 4
XA_HEAD_DIM = XA_WIDTH // XA_HEADS
N_MEM = 256
EPS = 1e-6
SPLIT_SIZES = (POOL_WIDTH, POOL_WIDTH, SSD_WIDTH, CONV_DIM, SSD_HEADS, XA_WIDTH, XA_WIDTH)
D_IN_PROJ = 2 * POOL_WIDTH + SSD_WIDTH + CONV_DIM + SSD_HEADS + 2 * XA_WIDTH

kernel_name = "hybrid_pool_ssd_memxattn_decoder_step"


def rmsnorm(x, g):
    xf = x.astype(jnp.float32)
    y = xf * lax.rsqrt(jnp.mean(xf * xf, axis=-1, keepdims=True) + EPS)
    return (y * g.astype(jnp.float32)).astype(x.dtype)


def split_cols(p, sizes):
    idx = [int(i) for i in np.cumsum(sizes)[:-1]]
    return jnp.split(p, idx, axis=-1)


def pool_mixer(u, state, start_pos, w_pool, s_pool):
    b, L, _ = u.shape
    s0 = POOL_STATE_LEN
    buf_raw = jnp.concatenate([state.astype(u.dtype), u], axis=1)
    buf = buf_raw.astype(jnp.float32)
    cs = jnp.concatenate([jnp.zeros((b, 1, POOL_WIDTH), jnp.float32),
                          jnp.cumsum(buf, axis=1)], axis=1)
    end = cs[:, s0 + 1:]
    pos = start_pos + jnp.arange(L, dtype=jnp.int32)
    means = []
    for gi, w in enumerate(POOL_WINDOWS):
        sl = slice(gi * POOL_GROUP_DIM, (gi + 1) * POOL_GROUP_DIM)
        start = cs[:, s0 + 1 - w: s0 + 1 - w + L, sl]
        cnt = jnp.minimum(pos + 1, w).astype(jnp.float32)
        means.append((end[..., sl] - start) / cnt[None, :, None])
    d = (jnp.concatenate(means, axis=-1) - buf[:, s0:]).reshape(b, L, N_POOL_GROUPS, POOL_GROUP_DIM)
    out = jnp.einsum('blgc,gcd->blgd', d, w_pool.astype(jnp.float32)).reshape(b, L, POOL_WIDTH)
    out = out * s_pool.astype(jnp.float32)
    return out.astype(u.dtype), buf_raw[:, -s0:]


def causal_dwconv(xbc, conv_state, conv_w, conv_b):
    buf = jnp.concatenate([conv_state.astype(xbc.dtype), xbc], axis=1)
    y = lax.conv_general_dilated(buf, conv_w.astype(xbc.dtype)[:, None, :], window_strides=(1,),
                                 padding='VALID', dimension_numbers=('NWC', 'WIO', 'NWC'),
                                 feature_group_count=CONV_DIM)
    return jax.nn.silu(y + conv_b.astype(xbc.dtype)), buf[:, -(CONV_WIDTH - 1):]


def ssd_scan(x, dt, A, Bm, Cm, s0):
    b, L = x.shape[0], x.shape[1]
    G, R, P, N = SSD_GROUPS, SSD_HEADS_PER_GROUP, SSD_HEAD_DIM, D_STATE
    q = CHUNK if L % CHUNK == 0 else L
    nc = L // q
    to_chunks = lambda a: jnp.moveaxis(a.reshape((b, nc, q) + a.shape[2:]), 1, 0)
    xc = to_chunks(x.astype(jnp.float32).reshape(b, L, G, R, P))
    dtc = to_chunks(dt.reshape(b, L, G, R))
    Bc = to_chunks(Bm.astype(jnp.float32))
    Cc = to_chunks(Cm.astype(jnp.float32))
    Ag = A.reshape(G, R)
    causal = jnp.tril(jnp.ones((q, q), dtype=bool))[None, :, :, None, None]

    def step(S, inp):
        xk, dtk, Bk, Ck = inp
        cum = jnp.cumsum(dtk * Ag, axis=1)
        seg = cum[:, :, None] - cum[:, None]
        Lm = jnp.exp(jnp.where(causal, seg, -jnp.inf))
        CB = jnp.einsum('bign,bjgn->bijg', Ck, Bk)
        M = CB[..., None] * Lm * dtk[:, None]
        y = jnp.einsum('bijgr,bjgrp->bigrp', M, xk)
        y = y + jnp.einsum('bign,bgrpn->bigrp', Ck, S) * jnp.exp(cum)[..., None]
        w_end = jnp.exp(cum[:, -1:] - cum) * dtk
        S_new = jnp.exp(cum[:, -1])[..., None, None] * S + jnp.einsum('bjgr,bjgrp,bjgn->bgrpn', w_end, xk, Bk)
        return S_new, y

    S0 = s0.astype(jnp.float32).reshape(b, G, R, P, N)
    S_fin, ys = lax.scan(step, S0, (xc, dtc, Bc, Cc))
    y = jnp.moveaxis(ys, 0, 1).reshape(b, L, SSD_HEADS, P)
    return y, S_fin.reshape(b, SSD_HEADS, P, N)


def mem_kv(mem, mem_norm_g, w_mem_k, w_mem_v):
    b = mem.shape[0]
    mn = rmsnorm(mem, mem_norm_g)
    k = (mn @ w_mem_k).reshape(b, N_MEM, XA_HEADS, XA_HEAD_DIM)
    v = (mn @ w_mem_v).reshape(b, N_MEM, XA_HEADS, XA_HEAD_DIM)
    return k, v


def layer(h, pool_st, conv_st, ssm_st, mk, mv, start_pos,
          norm_g, w_in, pool_w, pool_scale, conv_w, conv_b, dt_bias, a_log, d_skip, ssd_norm_g, w_out):
    b, L, _ = h.shape
    xn = rmsnorm(h, norm_g)
    u, g_pool, z, xbc, dt_raw, q, g_xa = split_cols(xn @ w_in, SPLIT_SIZES)
    pool_y, new_pool = pool_mixer(u, pool_st, start_pos, pool_w, pool_scale)
    pool_o = pool_y * jax.nn.silu(g_pool)
    xbc_c, new_conv = causal_dwconv(xbc, conv_st, conv_w, conv_b)
    xs, Bm, Cm = split_cols(xbc_c, (SSD_WIDTH, SSD_GROUPS * D_STATE, SSD_GROUPS * D_STATE))
    xs = xs.reshape(b, L, SSD_HEADS, SSD_HEAD_DIM)
    dt = jax.nn.softplus(dt_raw.astype(jnp.float32) + dt_bias.astype(jnp.float32))
    A = -jnp.exp(a_log.astype(jnp.float32))
    y, new_ssm = ssd_scan(xs, dt, A, Bm.reshape(b, L, SSD_GROUPS, D_STATE),
                          Cm.reshape(b, L, SSD_GROUPS, D_STATE), ssm_st)
    y = y + d_skip.astype(jnp.float32)[:, None] * xs.astype(jnp.float32)
    y = (y.reshape(b, L, SSD_WIDTH) * jax.nn.silu(z.astype(jnp.float32))).reshape(b, L, SSD_GROUPS, -1)
    y = y * lax.rsqrt(jnp.mean(y * y, axis=-1, keepdims=True) + EPS)
    ssd_o = (y.reshape(b, L, SSD_WIDTH) * ssd_norm_g.astype(jnp.float32)).astype(h.dtype)
    qh = q.reshape(b, L, XA_HEADS, XA_HEAD_DIM)
    scores = jnp.einsum('blhd,bmhd->bhlm', qh, mk.astype(qh.dtype)).astype(jnp.float32) / math.sqrt(XA_HEAD_DIM)
    probs = jax.nn.softmax(scores, axis=-1)
    xa = jnp.einsum('bhlm,bmhd->blhd', probs.astype(h.dtype), mv.astype(h.dtype)).reshape(b, L, XA_WIDTH)
    xa_o = xa * jax.nn.silu(g_xa)
    out = jnp.concatenate([pool_o, ssd_o, xa_o], axis=-1) @ w_out
    return h + out, new_pool, new_conv, new_ssm.astype(h.dtype)


def setup_inputs(seed: int = 0) -> dict:
    key = jax.random.key(seed)
    ks = jax.random.split(key, 24)
    nrm = lambda k, shape, s: jax.random.normal(k, shape, jnp.float32) * s
    dt0 = jnp.exp(jax.random.uniform(ks[0], (DEPTH, SSD_HEADS), jnp.float32, math.log(1e-3), math.log(1e-1)))
    return {
        "x_prompt": nrm(ks[1], (BATCH, SEQ, D_MODEL), 1.0),
        "x_sample": nrm(ks[2], (DEC_BATCH, DEC_SEQ, D_MODEL), 1.0),
        "mem_prompt": nrm(ks[3], (BATCH, N_MEM, D_MODEL), 1.0),
        "state_pool": nrm(ks[4], (DEPTH, DEC_BATCH, POOL_STATE_LEN, POOL_WIDTH), 1.0),
        "state_conv": nrm(ks[5], (DEPTH, DEC_BATCH, CONV_WIDTH - 1, CONV_DIM), 1.0),
        "state_ssm": nrm(ks[6], (DEPTH, DEC_BATCH, SSD_HEADS, SSD_HEAD_DIM, D_STATE), 0.5),
        "cache_mem_k": nrm(ks[7], (DEPTH, DEC_BATCH, N_MEM, XA_HEADS, XA_HEAD_DIM), 1.0),
        "cache_mem_v": nrm(ks[8], (DEPTH, DEC_BATCH, N_MEM, XA_HEADS, XA_HEAD_DIM), 1.0),
        "norm_g": 1.0 + nrm(ks[9], (DEPTH, D_MODEL), 0.02),
        "w_in": nrm(ks[10], (DEPTH, D_MODEL, D_IN_PROJ), D_MODEL ** -0.5),
        "pool_w": nrm(ks[11], (DEPTH, N_POOL_GROUPS, POOL_GROUP_DIM, POOL_GROUP_DIM), POOL_GROUP_DIM ** -0.5),
        "pool_scale": 1.0 + nrm(ks[12], (DEPTH, POOL_WIDTH), 0.02),
        "conv_w": nrm(ks[13], (DEPTH, CONV_WIDTH, CONV_DIM), CONV_WIDTH ** -0.5),
        "conv_b": nrm(ks[14], (DEPTH, CONV_DIM), 0.01),
        "dt_bias": dt0 + jnp.log(-jnp.expm1(-dt0)),
        "a_log": jnp.log(jax.random.uniform(ks[15], (DEPTH, SSD_HEADS), jnp.float32, 1.0, 16.0)),
        "d_skip": 1.0 + nrm(ks[16], (DEPTH, SSD_HEADS), 0.02),
        "ssd_norm_g": 1.0 + nrm(ks[17], (DEPTH, SSD_WIDTH), 0.02),
        "mem_norm_g": 1.0 + nrm(ks[18], (DEPTH, D_MODEL), 0.02),
        "w_mem_k": nrm(ks[19], (DEPTH, D_MODEL, XA_WIDTH), D_MODEL ** -0.5),
        "w_mem_v": nrm(ks[20], (DEPTH, D_MODEL, XA_WIDTH), D_MODEL ** -0.5),
        "w_out": nrm(ks[21], (DEPTH, D_MIX, D_MODEL), D_MIX ** -0.5),
        "final_norm_g": 1.0 + nrm(ks[22], (D_MODEL,), 0.02),
    }


def reference(x_prompt, x_sample, mem_prompt, state_pool, state_conv, state_ssm, cache_mem_k, cache_mem_v,
              norm_g, w_in, pool_w, pool_scale, conv_w, conv_b, dt_bias, a_log, d_skip, ssd_norm_g,
              mem_norm_g, w_mem_k, w_mem_v, w_out, final_norm_g):
    hp, hs = x_prompt, x_sample
    pool_p, conv_p, ssm_p, mk_p, mv_p = [], [], [], [], []
    pool_s, conv_s, ssm_s = [], [], []
    zp_pool = jnp.zeros((BATCH, POOL_STATE_LEN, POOL_WIDTH), x_prompt.dtype)
    zp_conv = jnp.zeros((BATCH, CONV_WIDTH - 1, CONV_DIM), x_prompt.dtype)
    zp_ssm = jnp.zeros((BATCH, SSD_HEADS, SSD_HEAD_DIM, D_STATE), x_prompt.dtype)
    for l in range(DEPTH):
        lw = (norm_g[l], w_in[l], pool_w[l], pool_scale[l], conv_w[l], conv_b[l],
              dt_bias[l], a_log[l], d_skip[l], ssd_norm_g[l], w_out[l])
        mk, mv = mem_kv(mem_prompt, mem_norm_g[l], w_mem_k[l], w_mem_v[l])
        hp, npool, nconv, nssm = layer(hp, zp_pool, zp_conv, zp_ssm, mk, mv, 0, *lw)
        pool_p.append(npool); conv_p.append(nconv); ssm_p.append(nssm); mk_p.append(mk); mv_p.append(mv)
        hs, npool, nconv, nssm = layer(hs, state_pool[l], state_conv[l], state_ssm[l],
                                       cache_mem_k[l], cache_mem_v[l], PAST_LEN, *lw)
        pool_s.append(npool); conv_s.append(nconv); ssm_s.append(nssm)
    y_prompt = rmsnorm(hp, final_norm_g)
    y_sample = rmsnorm(hs, final_norm_g)
    return (y_prompt, y_sample,
            jnp.stack(pool_p), jnp.stack(conv_p), jnp.stack(ssm_p), jnp.stack(mk_p), jnp.stack(mv_p),
            jnp.stack(pool_s), jnp.stack(conv_s), jnp.stack(ssm_s))
```

```python
import functools
import math

import jax
import jax.numpy as jnp
from jax import lax
from jax.experimental import pallas as pl
from jax.experimental.pallas import tpu as pltpu

F32 = jnp.float32
BF16 = jnp.bfloat16

D_MODEL = 1024
DEPTH = 4
POOL_WIDTH = 512
POOL_WINDOWS = (2, 4, 8, 16)
POOL_GROUP_DIM = 128
POOL_STATE_LEN = 15
SSD_WIDTH = 1024
SSD_HEAD_DIM = 64
SSD_HEADS = 16
SSD_GROUPS = 2
SSD_HEADS_PER_GROUP = 8
D_STATE = 128
CONV_WIDTH = 4
CONV_DIM = 1536
XA_WIDTH = 512
XA_HEADS = 4
XA_HEAD_DIM = 128
N_MEM = 256
D_MIX = 2048
EPS = 1e-6

LANES = 128
POOL_HIST = 16
CONV_HIST = 8

COL_U = 0
COL_GP = COL_U + POOL_WIDTH
COL_Z = COL_GP + POOL_WIDTH
COL_XBC = COL_Z + SSD_WIDTH
COL_Q = COL_XBC + CONV_DIM
COL_GX = COL_Q + XA_WIDTH
COL_DT = COL_GX + XA_WIDTH
DT_COPIES = LANES // SSD_HEADS
PROJ_WIDTH = COL_DT + LANES

VMEM_LIMIT_BYTES = 56 * 1024 * 1024


def _sigmoid(x):
    return 1.0 / (1.0 + jnp.exp(-x))


def _silu(x):
    return x * _sigmoid(x)


def _softplus(x):
    return jnp.maximum(x, 0.0) + jnp.log1p(jnp.exp(-jnp.abs(x)))


def _split3(v):
    hi = v.astype(BF16)
    r1 = v - hi.astype(F32)
    mid = r1.astype(BF16)
    lo = (r1 - mid.astype(F32)).astype(BF16)
    return hi, mid, lo


def _dot(a, b):
    return jnp.dot(a, b, preferred_element_type=F32)


def _dot_nt(a, b):
    return lax.dot_general(a, b, (((1,), (1,)), ((), ())), preferred_element_type=F32)


def _dot_tn(a, b):
    return lax.dot_general(a, b, (((0,), (0,)), ((), ())), preferred_element_type=F32)


def _exact_left_dot(sel_bf16, v):
    hi, mid, lo = _split3(v)
    return _dot(sel_bf16, hi) + _dot(sel_bf16, mid) + _dot(sel_bf16, lo)


def _exact_right_dot(v, sel_bf16):
    hi, mid, lo = _split3(v)
    return _dot(hi, sel_bf16) + _dot(mid, sel_bf16) + _dot(lo, sel_bf16)


def _norm_proj_kernel(x_ref, g_ref, w_ref, *o_refs, widths):
    x = x_ref[...]
    inv = lax.rsqrt(jnp.mean(x * x, axis=-1, keepdims=True) + EPS)
    xn = (x * inv * g_ref[...]).astype(BF16)
    off = 0
    for o_ref, w in zip(o_refs, widths):
        o_ref[...] = _dot(xn, w_ref[:, off:off + w])
        off += w


def _norm_proj(x, g, w_bf16, widths, tm):
    t, d = x.shape
    n = w_bf16.shape[1]
    assert sum(widths) == n and t % tm == 0
    return pl.pallas_call(
        functools.partial(_norm_proj_kernel, widths=widths),
        grid=(t // tm,),
        in_specs=[
            pl.BlockSpec((tm, d), lambda i: (i, 0)),
            pl.BlockSpec((1, d), lambda i: (0, 0)),
            pl.BlockSpec((d, n), lambda i: (0, 0)),
        ],
        out_specs=[pl.BlockSpec((tm, w), lambda i: (i, 0)) for w in widths],
        out_shape=[jax.ShapeDtypeStruct((t, w), F32) for w in widths],
        compiler_params=pltpu.CompilerParams(
            dimension_semantics=("arbitrary",), vmem_limit_bytes=VMEM_LIMIT_BYTES),
        name="norm_proj",
    )(x, g.reshape(1, d), w_bf16)


def _rmsnorm_kernel(x_ref, g_ref, o_ref):
    x = x_ref[...]
    inv = lax.rsqrt(jnp.mean(x * x, axis=-1, keepdims=True) + EPS)
    o_ref[...] = x * inv * g_ref[...]


def _rmsnorm(x, g, tm):
    t, d = x.shape
    return pl.pallas_call(
        _rmsnorm_kernel,
        grid=(t // tm,),
        in_specs=[pl.BlockSpec((tm, d), lambda i: (i, 0)), pl.BlockSpec((1, d), lambda i: (0, 0))],
        out_specs=pl.BlockSpec((tm, d), lambda i: (i, 0)),
        out_shape=jax.ShapeDtypeStruct((t, d), F32),
        compiler_params=pltpu.CompilerParams(dimension_semantics=("arbitrary",)),
        name="final_rmsnorm",
    )(x, g.reshape(1, d))


def _mixer_kernel(*refs, nb, seq, n_chunks, has_state, start_pos):
    if has_state:
        proj_ref, h_ref, k_ref, v_ref, pool0_ref, conv0_ref, ssm0_ref = refs[:7]
        rest = refs[7:]
    else:
        proj_ref, h_ref, k_ref, v_ref = refs[:4]
        pool0_ref = conv0_ref = ssm0_ref = None
        rest = refs[4:]
    (wout_ref, poolw_ref, pscale_ref, convw_ref, convb_ref, dtb_ref, arow_ref, dskip_ref,
     normg_ref, emat_ref,
     hout_ref, poolo_ref, convo_ref, ssm_ref,
     extp_ref, extc_ref, xc_ref, y_ref, mix_ref) = rest

    t = nb * seq
    c = pl.program_id(1)

    @pl.when(c == 0)
    def _init():
        if has_state:
            extp_ref[:, POOL_HIST - POOL_STATE_LEN:POOL_HIST, :] = pool0_ref[...]
            extc_ref[:, CONV_HIST - (CONV_WIDTH - 1):CONV_HIST, :] = conv0_ref[...]
            ssm_ref[...] = ssm0_ref[...]
        else:
            extp_ref[:, 0:POOL_HIST, :] = jnp.zeros((nb, POOL_HIST, POOL_WIDTH), F32)
            extc_ref[:, 0:CONV_HIST, :] = jnp.zeros((nb, CONV_HIST, CONV_DIM), F32)
            ssm_ref[...] = jnp.zeros(ssm_ref.shape, F32)

    extp_ref[:, POOL_HIST:POOL_HIST + seq, :] = proj_ref[:, COL_U:COL_U + POOL_WIDTH].reshape(
        nb, seq, POOL_WIDTH)
    tok = lax.broadcasted_iota(jnp.int32, (nb, seq, LANES), 1)
    pos1 = start_pos + c * seq + tok + 1
    for gi, w in enumerate(POOL_WINDOWS):
        sl = slice(gi * POOL_GROUP_DIM, (gi + 1) * POOL_GROUP_DIM)
        cur = extp_ref[:, POOL_HIST:POOL_HIST + seq, sl]
        acc = cur
        for k in range(1, w):
            acc = acc + extp_ref[:, POOL_HIST - k:POOL_HIST - k + seq, sl]
        cnt = jnp.minimum(pos1, w).astype(F32)
        d = (acc / cnt - cur).reshape(t, POOL_GROUP_DIM)
        o = _dot(d.astype(BF16), poolw_ref[gi]) * pscale_ref[:, sl]
        gp = proj_ref[:, COL_GP + gi * POOL_GROUP_DIM:COL_GP + (gi + 1) * POOL_GROUP_DIM]
        mix_ref[:, sl] = (o * _silu(gp)).astype(BF16)

    @pl.when(c == n_chunks - 1)
    def _pool_out():
        poolo_ref[...] = extp_ref[:, seq + POOL_HIST - POOL_STATE_LEN:seq + POOL_HIST, :]

    if n_chunks > 1:
        extp_ref[:, 0:POOL_HIST, :] = extp_ref[:, seq:seq + POOL_HIST, :]

    extc_ref[:, CONV_HIST:CONV_HIST + seq, :] = proj_ref[:, COL_XBC:COL_XBC + CONV_DIM].reshape(
        nb, seq, CONV_DIM)
    cblk = 512
    for j in range(CONV_DIM // cblk):
        sl = slice(j * cblk, (j + 1) * cblk)
        acc = convb_ref[:, sl][None]
        for k in range(CONV_WIDTH):
            r0 = CONV_HIST - (CONV_WIDTH - 1) + k
            acc = acc + convw_ref[k:k + 1, sl][None] * extc_ref[:, r0:r0 + seq, sl]
        xc_ref[:, sl] = _silu(acc).reshape(t, cblk)

    @pl.when(c == n_chunks - 1)
    def _conv_out():
        convo_ref[...] = extc_ref[:, seq + CONV_HIST - (CONV_WIDTH - 1):seq + CONV_HIST, :]

    if n_chunks > 1:
        extc_ref[:, CONV_HIST - (CONV_WIDTH - 1):CONV_HIST, :] = extc_ref[
            :, seq + CONV_HIST - (CONV_WIDTH - 1):seq + CONV_HIST, :]

    dt = _softplus(proj_ref[:, COL_DT:COL_DT + LANES] + dtb_ref[...])
    a = dt * arow_ref[...]
    ri = lax.broadcasted_iota(jnp.int32, (t, t), 0)
    ci = lax.broadcasted_iota(jnp.int32, (t, t), 1)
    if nb > 1:
        seq_start = ri - lax.rem(ri, seq)
        causal = jnp.bitwise_or(ci - seq_start, ri - ci) >= 0
    else:
        causal = ci <= ri
    tri = jnp.where(causal, 1.0, 0.0).astype(BF16)
    cum = _exact_left_dot(tri, a)
    pad = LANES - t
    if pad:
        zpad = jnp.zeros((pad, LANES), F32)
        cum_t = jnp.concatenate([cum, zpad], axis=0).T[:, 0:t]
        dt_t = jnp.concatenate([dt, zpad], axis=0).T[:, 0:t]
    else:
        cum_t = cum.T
        dt_t = dt.T
    cum3 = cum.reshape(nb, seq, LANES)
    cum_last = cum3[:, seq - 1:seq, :]
    e_tok = jnp.exp(cum)
    w_tok = (jnp.exp(cum_last - cum3) * dt.reshape(nb, seq, LANES)).reshape(t, LANES)
    dec = jnp.exp(cum_last)
    e_exp = _exact_right_dot(e_tok, emat_ref[...])
    w_exp = _exact_right_dot(w_tok, emat_ref[...])

    lane = lax.broadcasted_iota(jnp.int32, (t, LANES), 1)
    low_half = lane < SSD_HEAD_DIM

    for g in range(SSD_GROUPS):
        bg = xc_ref[:, SSD_WIDTH + g * D_STATE:SSD_WIDTH + (g + 1) * D_STATE].astype(BF16)
        cg = xc_ref[:, SSD_WIDTH + SSD_GROUPS * D_STATE + g * D_STATE:
                    SSD_WIDTH + SSD_GROUPS * D_STATE + (g + 1) * D_STATE].astype(BF16)
        cb = jnp.where(causal, _dot_nt(cg, bg), 0.0)
        for pp in range(SSD_HEADS_PER_GROUP // 2):
            h0 = g * SSD_HEADS_PER_GROUP + 2 * pp
            ms = []
            for hh in (h0, h0 + 1):
                seg = cum[:, hh:hh + 1] - cum_t[hh:hh + 1, :]
                ms.append((cb * jnp.exp(jnp.minimum(seg, 0.0)) * dt_t[hh:hh + 1, :]).astype(BF16))
            col = (h0 // 2) * LANES
            xpair = xc_ref[:, col:col + LANES]
            rhs = jnp.concatenate(
                [jnp.where(low_half, xpair, 0.0), jnp.where(low_half, 0.0, xpair)], axis=0).astype(BF16)
            y_ref[:, col:col + LANES] = _dot(jnp.concatenate(ms, axis=1), rhs)
        gsl = slice(g * 512, (g + 1) * 512)
        for b in range(nb):
            rows = slice(b * seq, (b + 1) * seq)
            s_old = ssm_ref[b, gsl, :]
            yi = _dot_nt(cg[rows], s_old.astype(BF16))
            y_ref[rows, gsl] = y_ref[rows, gsl] + yi * e_exp[rows, gsl]
            xw = (xc_ref[rows, gsl] * w_exp[rows, gsl]).astype(BF16)
            upd = _dot_tn(xw, bg[rows])
            for r in range(SSD_HEADS_PER_GROUP):
                hh = g * SSD_HEADS_PER_GROUP + r
                hs = slice(r * SSD_HEAD_DIM, (r + 1) * SSD_HEAD_DIM)
                ssm_ref[b, g * 512 + r * SSD_HEAD_DIM:g * 512 + (r + 1) * SSD_HEAD_DIM, :] = (
                    s_old[hs] * dec[b, :, hh:hh + 1] + upd[hs])

    z = proj_ref[:, COL_Z:COL_Z + SSD_WIDTH]
    y = (y_ref[...] + dskip_ref[...] * xc_ref[:, 0:SSD_WIDTH]) * _silu(z)
    gw = SSD_WIDTH // SSD_GROUPS
    for g in range(SSD_GROUPS):
        yg = y[:, g * gw:(g + 1) * gw]
        inv = lax.rsqrt(jnp.mean(yg * yg, axis=-1, keepdims=True) + EPS)
        mix_ref[:, POOL_WIDTH + g * gw:POOL_WIDTH + (g + 1) * gw] = (
            yg * inv * normg_ref[:, g * gw:(g + 1) * gw]).astype(BF16)

    inv_sqrt_d = 1.0 / math.sqrt(XA_HEAD_DIM)
    for hh in range(XA_HEADS):
        sl = slice(hh * XA_HEAD_DIM, (hh + 1) * XA_HEAD_DIM)
        q3 = proj_ref[:, COL_Q + hh * XA_HEAD_DIM:COL_Q + (hh + 1) * XA_HEAD_DIM].astype(BF16).reshape(
            nb, seq, XA_HEAD_DIM)
        k3 = k_ref[:, :, sl].astype(BF16)
        v3 = v_ref[:, :, sl].astype(BF16)
        s = lax.dot_general(q3, k3, (((2,), (2,)), ((0,), (0,))), preferred_element_type=F32) * inv_sqrt_d
        s = s - jnp.max(s, axis=-1, keepdims=True)
        p = jnp.exp(s)
        p = p / jnp.sum(p, axis=-1, keepdims=True)
        o = lax.dot_general(p.astype(BF16), v3, (((2,), (1,)), ((0,), (0,))), preferred_element_type=F32)
        gx = proj_ref[:, COL_GX + hh * XA_HEAD_DIM:COL_GX + (hh + 1) * XA_HEAD_DIM]
        mix_ref[:, POOL_WIDTH + SSD_WIDTH + hh * XA_HEAD_DIM:
                POOL_WIDTH + SSD_WIDTH + (hh + 1) * XA_HEAD_DIM] = (
            o.reshape(t, XA_HEAD_DIM) * _silu(gx)).astype(BF16)

    hout_ref[...] = h_ref[...] + _dot(mix_ref[...], wout_ref[...])


def _mixer(proj, h, mem_k, mem_v, states, lw, *, nb, seq, n_chunks, start_pos):
    t_all = h.shape[0]
    t = nb * seq
    n_seq = t_all // (seq * n_chunks)
    n_outer = n_seq // nb
    assert n_outer * nb == n_seq and n_seq * seq * n_chunks == t_all
    has_state = states is not None

    row_map = lambda i, c: (i * n_chunks + c, 0)
    seq_map3 = lambda i, c: (i, 0, 0)
    full2 = lambda i, c: (0, 0)
    full3 = lambda i, c: (0, 0, 0)

    in_specs = [
        pl.BlockSpec((t, PROJ_WIDTH), row_map),
        pl.BlockSpec((t, D_MODEL), row_map),
        pl.BlockSpec((nb, N_MEM, XA_WIDTH), seq_map3),
        pl.BlockSpec((nb, N_MEM, XA_WIDTH), seq_map3),
    ]
    args = [proj, h, mem_k, mem_v]
    if has_state:
        in_specs += [
            pl.BlockSpec((nb, POOL_STATE_LEN, POOL_WIDTH), seq_map3),
            pl.BlockSpec((nb, CONV_WIDTH - 1, CONV_DIM), seq_map3),
            pl.BlockSpec((nb, SSD_WIDTH, D_STATE), seq_map3),
        ]
        args += list(states)
    in_specs += [
        pl.BlockSpec((D_MIX, D_MODEL), full2),
        pl.BlockSpec((4, POOL_GROUP_DIM, POOL_GROUP_DIM), full3),
        pl.BlockSpec((1, POOL_WIDTH), full2),
        pl.BlockSpec((CONV_WIDTH, CONV_DIM), full2),
        pl.BlockSpec((1, CONV_DIM), full2),
        pl.BlockSpec((1, LANES), full2),
        pl.BlockSpec((1, LANES), full2),
        pl.BlockSpec((1, SSD_WIDTH), full2),
        pl.BlockSpec((1, SSD_WIDTH), full2),
        pl.BlockSpec((LANES, SSD_WIDTH), full2),
    ]
    args += list(lw)

    out_specs = [
        pl.BlockSpec((t, D_MODEL), row_map),
        pl.BlockSpec((nb, POOL_STATE_LEN, POOL_WIDTH), seq_map3),
        pl.BlockSpec((nb, CONV_WIDTH - 1, CONV_DIM), seq_map3),
        pl.BlockSpec((nb, SSD_WIDTH, D_STATE), seq_map3),
    ]
    out_shape = [
        jax.ShapeDtypeStruct((t_all, D_MODEL), F32),
        jax.ShapeDtypeStruct((n_seq, POOL_STATE_LEN, POOL_WIDTH), F32),
        jax.ShapeDtypeStruct((n_seq, CONV_WIDTH - 1, CONV_DIM), F32),
        jax.ShapeDtypeStruct((n_seq, SSD_WIDTH, D_STATE), F32),
    ]
    scratch = [
        pltpu.VMEM((nb, POOL_HIST + seq, POOL_WIDTH), F32),
        pltpu.VMEM((nb, CONV_HIST + seq, CONV_DIM), F32),
        pltpu.VMEM((t, CONV_DIM), F32),
        pltpu.VMEM((t, SSD_WIDTH), F32),
        pltpu.VMEM((t, D_MIX), BF16),
    ]
    return pl.pallas_call(
        functools.partial(_mixer_kernel, nb=nb, seq=seq, n_chunks=n_chunks, has_state=has_state,
                          start_pos=start_pos),
        grid=(n_outer, n_chunks),
        in_specs=in_specs,
        out_specs=out_specs,
        out_shape=out_shape,
        scratch_shapes=scratch,
        compiler_params=pltpu.CompilerParams(
            dimension_semantics=("arbitrary", "arbitrary"), vmem_limit_bytes=VMEM_LIMIT_BYTES),
        name="mixer_state" if has_state else "mixer_fresh",
    )(*args)


def _reorder_w_in(w_in):
    o = 0
    parts = {}
    for name, width in (("u", POOL_WIDTH), ("gp", POOL_WIDTH), ("z", SSD_WIDTH), ("xbc", CONV_DIM),
                        ("dt", SSD_HEADS), ("q", XA_WIDTH), ("gx", XA_WIDTH)):
        parts[name] = w_in[..., o:o + width]
        o += width
    dt_rep = jnp.concatenate([parts["dt"]] * DT_COPIES, axis=-1)
    return jnp.concatenate(
        [parts["u"], parts["gp"], parts["z"], parts["xbc"], parts["q"], parts["gx"], dt_rep],
        axis=-1).astype(BF16)


def kernel(x_prompt, x_sample, mem_prompt, state_pool, state_conv, state_ssm, cache_mem_k, cache_mem_v,
           norm_g, w_in, pool_w, pool_scale, conv_w, conv_b, dt_bias, a_log, d_skip, ssd_norm_g,
           mem_norm_g, w_mem_k, w_mem_v, w_out, final_norm_g):
    batch, seq_p, _ = x_prompt.shape
    dec_batch, seq_s, _ = x_sample.shape
    past_len = 16384

    w_in_r = _reorder_w_in(w_in)
    w_kv = jnp.concatenate([w_mem_k, w_mem_v], axis=-1).astype(BF16)
    w_out_b = w_out.astype(BF16)
    pool_w_b = pool_w.astype(BF16)
    dtb_rep = jnp.tile(dt_bias, (1, DT_COPIES)).reshape(DEPTH, 1, LANES)
    a_rep = jnp.tile(-jnp.exp(a_log), (1, DT_COPIES)).reshape(DEPTH, 1, LANES)
    dskip_ch = jnp.repeat(d_skip, SSD_HEAD_DIM, axis=-1).reshape(DEPTH, 1, SSD_WIDTH)
    head_of_ch = jnp.arange(SSD_WIDTH, dtype=jnp.int32) // SSD_HEAD_DIM
    emat = (jnp.arange(LANES, dtype=jnp.int32)[:, None] == head_of_ch[None, :]).astype(BF16)

    hp = x_prompt.reshape(batch * seq_p, D_MODEL)
    hs = x_sample.reshape(dec_batch * seq_s, D_MODEL)
    mem2d = mem_prompt.reshape(batch * N_MEM, D_MODEL)
    st_ssm = state_ssm.reshape(DEPTH, dec_batch, SSD_WIDTH, D_STATE)
    ck = cache_mem_k.reshape(DEPTH, dec_batch, N_MEM, XA_WIDTH)
    cv = cache_mem_v.reshape(DEPTH, dec_batch, N_MEM, XA_WIDTH)

    outs = {k: [] for k in ("pool_p", "conv_p", "ssm_p", "mk_p", "mv_p", "pool_s", "conv_s", "ssm_s")}
    for l in range(DEPTH):
        lw = (w_out_b[l], pool_w_b[l], pool_scale[l].reshape(1, POOL_WIDTH), conv_w[l],
              conv_b[l].reshape(1, CONV_DIM), dtb_rep[l], a_rep[l], dskip_ch[l],
              ssd_norm_g[l].reshape(1, SSD_WIDTH), emat)
        mk, mv = _norm_proj(mem2d, mem_norm_g[l], w_kv[l], (XA_WIDTH, XA_WIDTH), 256)
        mk = mk.reshape(batch, N_MEM, XA_WIDTH)
        mv = mv.reshape(batch, N_MEM, XA_WIDTH)
        (proj_p,) = _norm_proj(hp, norm_g[l], w_in_r[l], (PROJ_WIDTH,), 256)
        (proj_s,) = _norm_proj(hs, norm_g[l], w_in_r[l], (PROJ_WIDTH,), 256)
        hp, pool_p, conv_p, ssm_p = _mixer(
            proj_p, hp, mk, mv, None, lw, nb=1, seq=128, n_chunks=seq_p // 128, start_pos=0)
        hs, pool_s, conv_s, ssm_s = _mixer(
            proj_s, hs, ck[l], cv[l], (state_pool[l], state_conv[l], st_ssm[l]), lw,
            nb=8, seq=seq_s, n_chunks=1, start_pos=past_len)
        outs["pool_p"].append(pool_p)
        outs["conv_p"].append(conv_p)
        outs["ssm_p"].append(ssm_p.reshape(batch, SSD_HEADS, SSD_HEAD_DIM, D_STATE))
        outs["mk_p"].append(mk.reshape(batch, N_MEM, XA_HEADS, XA_HEAD_DIM))
        outs["mv_p"].append(mv.reshape(batch, N_MEM, XA_HEADS, XA_HEAD_DIM))
        outs["pool_s"].append(pool_s)
        outs["conv_s"].append(conv_s)
        outs["ssm_s"].append(ssm_s.reshape(dec_batch, SSD_HEADS, SSD_HEAD_DIM, D_STATE))

    y_prompt = _rmsnorm(hp, final_norm_g, 512).reshape(batch, seq_p, D_MODEL)
    y_sample = _rmsnorm(hs, final_norm_g, 512).reshape(dec_batch, seq_s, D_MODEL)
    return (y_prompt, y_sample,
            jnp.stack(outs["pool_p"]), jnp.stack(outs["conv_p"]), jnp.stack(outs["ssm_p"]),
            jnp.stack(outs["mk_p"]), jnp.stack(outs["mv_p"]),
            jnp.stack(outs["pool_s"]), jnp.stack(outs["conv_s"]), jnp.stack(outs["ssm_s"]))
```

```python
import functools
import math

import jax
import jax.numpy as jnp
from jax import lax
from jax.experimental import pallas as pl
from jax.experimental.pallas import tpu as pltpu

F32 = jnp.float32
BF16 = jnp.bfloat16

D_MODEL = 1024
DEPTH = 4
PAST_LEN = 16384
POOL_WIDTH = 512
POOL_WINDOWS = (2, 4, 8, 16)
POOL_GROUP_DIM = 128
POOL_STATE_LEN = 15
SSD_WIDTH = 1024
SSD_HEAD_DIM = 64
SSD_HEADS = 16
SSD_GROUPS = 2
SSD_HEADS_PER_GROUP = 8
SSD_GROUP_WIDTH = SSD_WIDTH // SSD_GROUPS
D_STATE = 128
CONV_WIDTH = 4
CONV_DIM = 1536
XA_WIDTH = 512
XA_HEADS = 4
XA_HEAD_DIM = 128
N_MEM = 256
D_MIX = 2048
EPS = 1e-6
LOG2E = 1.4426950408889634

LANES = 128
SUBLANES = 8
POOL_HIST = 32
CONV_HIST = 8
CHUNK = 128
PROJ_PIECE = 512

COL_U = 0
COL_GP = COL_U + POOL_WIDTH
COL_Z = COL_GP + POOL_WIDTH
COL_XBC = COL_Z + SSD_WIDTH
COL_Q = COL_XBC + CONV_DIM
COL_GX = COL_Q + XA_WIDTH
COL_DT = COL_GX + XA_WIDTH
DT_COPIES = LANES // SSD_HEADS
PROJ_WIDTH = COL_DT + LANES

EXPAND_GROUPS = (0, 0, 1, 1, 2, 2, 0, 1)
EXPAND_TERM = (0, 1, 0, 1, 0, 1, 2, 2)
N_EXPAND = 3

VMEM_LIMIT_BYTES = 56 * 1024 * 1024


def _silu(x):
    h = 0.5 * x
    return h + h * jnp.tanh(h)


def _softplus(x):
    return jnp.maximum(x, 0.0) + jnp.log1p(jnp.exp(-jnp.abs(x)))


def _bf16_terms(v, n):
    terms = []
    for _ in range(n):
        t = v.astype(BF16).astype(F32)
        terms.append(t)
        v = v - t
    return terms


def _dot(a, b):
    return jnp.dot(a, b, preferred_element_type=F32)


def _dot_nt(a, b):
    return lax.dot_general(a, b, (((1,), (1,)), ((), ())), preferred_element_type=F32)


def _dot_tn(a, b):
    return lax.dot_general(a, b, (((0,), (0,)), ((), ())), preferred_element_type=F32)


def _exact_left_dot(sel_bf16, v):
    return sum(_dot(sel_bf16, t.astype(BF16)) for t in _bf16_terms(v, 3))


def _rms_scale(x, g):
    return x * lax.rsqrt(jnp.mean(x * x, axis=-1, keepdims=True) + EPS) * g


def _init_history(extp_ref, extc_ref, ssm_ref, init_refs):
    nb = extp_ref.shape[0]
    extp_ref[:, 0:POOL_HIST, :] = jnp.zeros((nb, POOL_HIST, POOL_WIDTH), F32)
    if init_refs is None:
        extc_ref[:, 0:CONV_HIST, :] = jnp.zeros((nb, CONV_HIST, CONV_DIM), F32)
        ssm_ref[...] = jnp.zeros(ssm_ref.shape, F32)
    else:
        pool0_ref, conv0_ref, ssm0_ref = init_refs
        extp_ref[:, POOL_HIST - POOL_STATE_LEN:POOL_HIST, :] = pool0_ref[...]
        extc_ref[:, CONV_HIST - (CONV_WIDTH - 1):CONV_HIST, :] = conv0_ref[...]
        ssm_ref[...] = ssm0_ref[...]


def _no_fill(n=1):
    del n


def _pool_heads(proj_ref, extp_ref, pa_ref, pb_ref, mix_ref, poolw_ref, pscale_ref, nb, seq, pos0,
                fill=_no_fill):
    t = nb * seq
    n = POOL_HIST + seq
    extp_ref[:, POOL_HIST:n, :] = proj_ref[:, COL_U:COL_U + POOL_WIDTH].reshape(nb, seq, POOL_WIDTH)
    g = POOL_GROUP_DIM
    pa_ref[:, 8:n, :] = extp_ref[:, 8:n, :] + extp_ref[:, 7:n - 1, :]
    fill()
    pb_ref[:, 16:n, g:4 * g] = pa_ref[:, 16:n, g:4 * g] + pa_ref[:, 14:n - 2, g:4 * g]
    pa_ref[:, 24:n, 2 * g:4 * g] = pb_ref[:, 24:n, 2 * g:4 * g] + pb_ref[:, 20:n - 4, 2 * g:4 * g]
    pb_ref[:, 32:n, 3 * g:4 * g] = pa_ref[:, 32:n, 3 * g:4 * g] + pa_ref[:, 24:n - 8, 3 * g:4 * g]
    tok = lax.broadcasted_iota(jnp.int32, (nb, seq, LANES), 1)
    pos1 = pos0 + tok + 1
    for gi, w in enumerate(POOL_WINDOWS):
        sl = slice(gi * g, (gi + 1) * g)
        src = pa_ref if gi % 2 == 0 else pb_ref
        cnt = jnp.minimum(pos1, w).astype(F32)
        d = (src[:, POOL_HIST:n, sl] / cnt - extp_ref[:, POOL_HIST:n, sl]).reshape(t, g)
        o = _dot(d.astype(BF16), poolw_ref[gi]) * pscale_ref[:, sl]
        gp = proj_ref[:, COL_GP + gi * g:COL_GP + (gi + 1) * g]
        mix_ref[:, sl] = (o * _silu(gp)).astype(BF16)
    fill()


def _conv_silu(proj_ref, extc_ref, xc_ref, convw_ref, convb_ref, nb, seq, fill=_no_fill):
    t = nb * seq
    extc_ref[:, CONV_HIST:CONV_HIST + seq, :] = proj_ref[:, COL_XBC:COL_XBC + CONV_DIM].reshape(
        nb, seq, CONV_DIM)
    cblk = 512
    for j in range(CONV_DIM // cblk):
        sl = slice(j * cblk, (j + 1) * cblk)
        acc = convb_ref[:, sl][None]
        for k in range(CONV_WIDTH):
            r0 = CONV_HIST - (CONV_WIDTH - 1) + k
            acc = acc + convw_ref[k:k + 1, sl][None] * extc_ref[:, r0:r0 + seq, sl]
        xc_ref[:, sl] = _silu(acc).reshape(t, cblk)
        fill()


def _expand_heads(vals, sel_ref):
    t = vals[0].shape[0]
    terms = [_bf16_terms(v, 1 + max(tm for gq, tm in zip(EXPAND_GROUPS, EXPAND_TERM) if gq == i))
             for i, v in enumerate(vals)]
    group = lax.shift_right_logical(lax.broadcasted_iota(jnp.int32, (t, LANES), 1), 4)
    packed = terms[EXPAND_GROUPS[0]][EXPAND_TERM[0]]
    for gidx in range(1, len(EXPAND_GROUPS)):
        packed = jnp.where(group == gidx, terms[EXPAND_GROUPS[gidx]][EXPAND_TERM[gidx]], packed)
    out = _dot(packed.astype(BF16), sel_ref[...])
    return [out[:, i * SSD_WIDTH:(i + 1) * SSD_WIDTH] for i in range(len(vals))]


def _ssd_chunk(proj_ref, xc_ref, y_ref, ssm_ref, dtb_ref, arow_ref, sel_ref, nb, seq, fill=_no_fill):
    t = nb * seq
    dt = _softplus(proj_ref[:, COL_DT:COL_DT + LANES] + dtb_ref[...])
    a = dt * arow_ref[...]
    ri = lax.broadcasted_iota(jnp.int32, (t, t), 0)
    ci = lax.broadcasted_iota(jnp.int32, (t, t), 1)
    if nb > 1:
        seq_start = ri - lax.rem(ri, seq)
        causal = jnp.bitwise_or(ci - seq_start, ri - ci) >= 0
    else:
        causal = ci <= ri
    tri = jnp.where(causal, 1.0, 0.0).astype(BF16)
    cum = _exact_left_dot(tri, a)
    cum2 = cum * LOG2E
    pad = LANES - t
    if pad:
        cum2_t = jnp.concatenate([cum2, jnp.zeros((pad, LANES), F32)], axis=0).T[:, 0:t]
    else:
        cum2_t = cum2.T
    cum3 = cum.reshape(nb, seq, LANES)
    cum_last = cum3[:, seq - 1:seq, :]
    e_tok = jnp.exp(cum)
    w_tok = (jnp.exp(cum_last - cum3) * dt.reshape(nb, seq, LANES)).reshape(t, LANES)
    dec = jnp.exp(cum_last)
    e_exp, w_exp, dt_exp = _expand_heads([e_tok, w_tok, dt], sel_ref)
    fill()

    lane = lax.broadcasted_iota(jnp.int32, (t, LANES), 1)
    low_half = lane < SSD_HEAD_DIM

    for g in range(SSD_GROUPS):
        b0 = SSD_WIDTH + g * D_STATE
        c0 = SSD_WIDTH + SSD_GROUPS * D_STATE + g * D_STATE
        bg = xc_ref[:, b0:b0 + D_STATE].astype(BF16)
        cg = xc_ref[:, c0:c0 + D_STATE].astype(BF16)
        cb = jnp.where(causal, _dot_nt(cg, bg), 0.0)
        for pp in range(SSD_HEADS_PER_GROUP // 2):
            h0 = g * SSD_HEADS_PER_GROUP + 2 * pp
            ms = []
            for hh in (h0, h0 + 1):
                seg2 = cum2[:, hh:hh + 1] - cum2_t[hh:hh + 1, :]
                ms.append((cb * jnp.exp2(jnp.minimum(seg2, 0.0))).astype(BF16))
            col = (h0 // 2) * LANES
            xdt = xc_ref[:, col:col + LANES] * dt_exp[:, col:col + LANES]
            rhs = jnp.concatenate(
                [jnp.where(low_half, xdt, 0.0), jnp.where(low_half, 0.0, xdt)], axis=0).astype(BF16)
            y_ref[:, col:col + LANES] = _dot(jnp.concatenate(ms, axis=1), rhs)
            if pp % 2 == 1:
                fill()
        gsl = slice(g * SSD_GROUP_WIDTH, (g + 1) * SSD_GROUP_WIDTH)
        for b in range(nb):
            rows = slice(b * seq, (b + 1) * seq)
            s_old = ssm_ref[b, gsl, :]
            yi = _dot_nt(cg[rows], s_old.astype(BF16))
            y_ref[rows, gsl] = y_ref[rows, gsl] + yi * e_exp[rows, gsl]
            xw = (xc_ref[rows, gsl] * w_exp[rows, gsl]).astype(BF16)
            upd = _dot_tn(xw, bg[rows])
            for r in range(SSD_HEADS_PER_GROUP):
                hh = g * SSD_HEADS_PER_GROUP + r
                hs = slice(r * SSD_HEAD_DIM, (r + 1) * SSD_HEAD_DIM)
                r0 = g * SSD_GROUP_WIDTH + r * SSD_HEAD_DIM
                ssm_ref[b, r0:r0 + SSD_HEAD_DIM, :] = s_old[hs] * dec[b, :, hh:hh + 1] + upd[hs]


def _gated_group_norm(proj_ref, xc_ref, y_ref, mix_ref, dskip_ref, normg_ref, fill=_no_fill):
    z = proj_ref[:, COL_Z:COL_Z + SSD_WIDTH]
    y = (y_ref[...] + dskip_ref[...] * xc_ref[:, 0:SSD_WIDTH]) * _silu(z)
    for g in range(SSD_GROUPS):
        gsl = slice(g * SSD_GROUP_WIDTH, (g + 1) * SSD_GROUP_WIDTH)
        mix_ref[:, POOL_WIDTH + g * SSD_GROUP_WIDTH:POOL_WIDTH + (g + 1) * SSD_GROUP_WIDTH] = (
            _rms_scale(y[:, gsl], normg_ref[:, gsl])).astype(BF16)
    fill()


def _softmax_rows(s):
    s = s - jnp.max(s, axis=-1, keepdims=True)
    p = jnp.exp(s)
    return p / jnp.sum(p, axis=-1, keepdims=True)


def _xattn_dense(proj_ref, mix_ref, k_ref, v_ref):
    inv_sqrt_d = 1.0 / math.sqrt(XA_HEAD_DIM)
    for hh in range(XA_HEADS):
        sl = slice(hh * XA_HEAD_DIM, (hh + 1) * XA_HEAD_DIM)
        q = proj_ref[:, COL_Q + hh * XA_HEAD_DIM:COL_Q + (hh + 1) * XA_HEAD_DIM].astype(BF16)
        p = _softmax_rows(_dot_nt(q, k_ref[:, sl]) * inv_sqrt_d)
        o = _dot(p.astype(BF16), v_ref[:, sl])
        gx = proj_ref[:, COL_GX + hh * XA_HEAD_DIM:COL_GX + (hh + 1) * XA_HEAD_DIM]
        m0 = POOL_WIDTH + SSD_WIDTH + hh * XA_HEAD_DIM
        mix_ref[:, m0:m0 + XA_HEAD_DIM] = (o * _silu(gx)).astype(BF16)


def _xattn_rowmajor(proj_ref, mix_ref, k_ref, v_ref, nb, seq):
    t = nb * seq
    inv_sqrt_d = 1.0 / math.sqrt(XA_HEAD_DIM)
    q4 = jnp.concatenate(
        [proj_ref[:, COL_Q + hh * XA_HEAD_DIM:COL_Q + (hh + 1) * XA_HEAD_DIM].astype(BF16).reshape(
            nb, seq, XA_HEAD_DIM) for hh in range(XA_HEADS)], axis=1)
    kf = k_ref[...].astype(BF16)
    vf = v_ref[...].astype(BF16)
    s = lax.dot_general(q4, kf, (((2,), (2,)), ((0,), (0,))), preferred_element_type=F32) * inv_sqrt_d
    shape = (nb, XA_HEADS * seq, N_MEM * XA_HEADS)
    row = lax.broadcasted_iota(jnp.int32, shape, 1)
    col = lax.broadcasted_iota(jnp.int32, shape, 2)
    own_head = jnp.bitwise_and(col, XA_HEADS - 1) * seq
    valid = jnp.bitwise_or(row - own_head, own_head + (seq - 1) - row) >= 0
    p = _softmax_rows(jnp.where(valid, s, -1e30))
    o4 = lax.dot_general(p.astype(BF16), vf, (((2,), (1,)), ((0,), (0,))), preferred_element_type=F32)
    for hh in range(XA_HEADS):
        o = o4[:, hh * seq:(hh + 1) * seq, :].reshape(t, XA_HEAD_DIM)
        gx = proj_ref[:, COL_GX + hh * XA_HEAD_DIM:COL_GX + (hh + 1) * XA_HEAD_DIM]
        m0 = POOL_WIDTH + SSD_WIDTH + hh * XA_HEAD_DIM
        mix_ref[:, m0:m0 + XA_HEAD_DIM] = (o * _silu(gx)).astype(BF16)


def _mem_kv_kernel(mem_ref, g_ref, w_ref, kf_ref, vf_ref, kd_ref, vd_ref):
    xn = _rms_scale(mem_ref[...], g_ref[...]).astype(BF16)
    kv = _dot(xn, w_ref[...])
    kd_ref[...] = kv[:, 0:XA_WIDTH].astype(BF16)
    vd_ref[...] = kv[:, XA_WIDTH:2 * XA_WIDTH].astype(BF16)
    for hh in range(XA_HEADS):
        kf_ref[pl.ds(hh, N_MEM, stride=XA_HEADS), :] = kv[:, hh * XA_HEAD_DIM:(hh + 1) * XA_HEAD_DIM]
        vf_ref[pl.ds(hh, N_MEM, stride=XA_HEADS), :] = kv[
            :, XA_WIDTH + hh * XA_HEAD_DIM:XA_WIDTH + (hh + 1) * XA_HEAD_DIM]


def _mem_kv(mem2d, mem_norm_g, w_kv):
    batch = mem2d.shape[0] // N_MEM
    flat = jax.ShapeDtypeStruct((DEPTH, batch, N_MEM * XA_HEADS, XA_HEAD_DIM), F32)
    dense = jax.ShapeDtypeStruct((DEPTH, batch, N_MEM, XA_WIDTH), BF16)
    flat_spec = pl.BlockSpec((None, None, N_MEM * XA_HEADS, XA_HEAD_DIM), lambda l, b: (l, b, 0, 0))
    dense_spec = pl.BlockSpec((None, None, N_MEM, XA_WIDTH), lambda l, b: (l, b, 0, 0))
    return pl.pallas_call(
        _mem_kv_kernel,
        grid=(DEPTH, batch),
        in_specs=[
            pl.BlockSpec((N_MEM, D_MODEL), lambda l, b: (b, 0)),
            pl.BlockSpec((None, 1, D_MODEL), lambda l, b: (l, 0, 0)),
            pl.BlockSpec((None, D_MODEL, 2 * XA_WIDTH), lambda l, b: (l, 0, 0)),
        ],
        out_specs=[flat_spec, flat_spec, dense_spec, dense_spec],
        out_shape=[flat, flat, dense, dense],
        compiler_params=pltpu.CompilerParams(dimension_semantics=("arbitrary", "arbitrary")),
        name="mem_kv",
    )(mem2d, mem_norm_g.reshape(DEPTH, 1, D_MODEL), w_kv)


def _prompt_kernel(x0_ref, xnext_ref, k_ref, v_ref, ng_ref, win_ref, wout_ref, poolw_ref, pscale_ref,
                   convw_ref, convb_ref, dtb_ref, arow_ref, dskip_ref, normg_ref, fng_ref, sel_ref,
                   y_out_ref, poolo_ref, convo_ref, ssm_ref,
                   h_ref, xn_ref, proj_a_ref, proj_b_ref, extp_ref, pa_ref, pb_ref, extc_ref, xc_ref,
                   y_ref, mix_ref, *, n_chunks):
    layer = pl.program_id(1)
    c = pl.program_id(2)
    slot = lax.rem(c, 2)
    rows = pl.ds(pl.multiple_of(c * CHUNK, CHUNK), CHUNK)
    nxt = lax.rem(c + 1, n_chunks)
    next_rows = pl.ds(pl.multiple_of(nxt * CHUNK, CHUNK), CHUNK)

    def project(hblk):
        return _dot(_rms_scale(hblk, ng_ref[...]).astype(BF16), win_ref[...])

    @pl.when(jnp.logical_and(layer == 0, c == 0))
    def _first():
        h_ref[0:CHUNK, :] = x0_ref[...]
        proj_a_ref[...] = project(x0_ref[...])

    @pl.when(jnp.logical_and(layer == 0, c < n_chunks - 1))
    def _load():
        h_ref[next_rows, :] = xnext_ref[...]

    @pl.when(c == 0)
    def _init():
        _init_history(extp_ref, extc_ref, ssm_ref, None)

    def step(proj_ref, proj_next_ref):
        xn_ref[...] = _rms_scale(h_ref[next_rows, :], ng_ref[...]).astype(BF16)

        def piece(c0, c1):
            def emit():
                proj_next_ref[:, c0:c1] = _dot(xn_ref[...], win_ref[:, c0:c1])
            return emit

        bounds = list(range(0, PROJ_WIDTH, PROJ_PIECE)) + [PROJ_WIDTH]
        pending = [piece(c0, c1) for c0, c1 in zip(bounds[:-1], bounds[1:])]

        def fill(n=1):
            for _ in range(min(n, len(pending))):
                pending.pop(0)()

        fill()
        _pool_heads(proj_ref, extp_ref, pa_ref, pb_ref, mix_ref, poolw_ref, pscale_ref, 1, CHUNK,
                    c * CHUNK, fill)
        poolo_ref[...] = extp_ref[:, CHUNK + POOL_HIST - POOL_STATE_LEN:CHUNK + POOL_HIST, :]
        extp_ref[:, 0:POOL_HIST, :] = extp_ref[:, CHUNK:CHUNK + POOL_HIST, :]

        _conv_silu(proj_ref, extc_ref, xc_ref, convw_ref, convb_ref, 1, CHUNK, fill)
        convo_ref[...] = extc_ref[:, CHUNK + CONV_HIST - (CONV_WIDTH - 1):CHUNK + CONV_HIST, :]
        extc_ref[:, CONV_HIST - (CONV_WIDTH - 1):CONV_HIST, :] = extc_ref[
            :, CHUNK + CONV_HIST - (CONV_WIDTH - 1):CHUNK + CONV_HIST, :]

        _ssd_chunk(proj_ref, xc_ref, y_ref, ssm_ref, dtb_ref, arow_ref, sel_ref, 1, CHUNK, fill)
        _gated_group_norm(proj_ref, xc_ref, y_ref, mix_ref, dskip_ref, normg_ref, fill)
        _xattn_dense(proj_ref, mix_ref, k_ref, v_ref)
        fill(len(pending))

        h_new = h_ref[rows, :] + _dot(mix_ref[...], wout_ref[...])
        h_ref[rows, :] = h_new

        @pl.when(layer == DEPTH - 1)
        def _final():
            y_out_ref[...] = _rms_scale(h_new, fng_ref[...])

    @pl.when(slot == 0)
    def _even():
        step(proj_a_ref, proj_b_ref)

    @pl.when(slot == 1)
    def _odd():
        step(proj_b_ref, proj_a_ref)


def _prompt_stack(x2d, k_dense, v_dense, ng, w_in_r, lw, fng, sel, batch, seq_len):
    n_chunks = seq_len // CHUNK
    assert n_chunks * CHUNK == seq_len and n_chunks % 2 == 0
    last = DEPTH - 1

    def lay3(shape):
        return pl.BlockSpec((None,) + shape, lambda b, l, c: (l, 0, 0))

    def next_lay3(shape):
        return pl.BlockSpec(
            (None,) + shape,
            lambda b, l, c: (jnp.minimum(l + jnp.where(c == n_chunks - 1, 1, 0), last), 0, 0))

    def per_seq(shape):
        return pl.BlockSpec((None, None) + shape, lambda b, l, c: (l, b, 0, 0))

    def state_out(shape):
        return pl.BlockSpec((None, 1) + shape, lambda b, l, c: (l, b, 0, 0))

    x0_spec = pl.BlockSpec((CHUNK, D_MODEL), lambda b, l, c: (b * n_chunks, 0))
    xnext_spec = pl.BlockSpec(
        (CHUNK, D_MODEL),
        lambda b, l, c: (b * n_chunks + jnp.where(l == 0, jnp.minimum(c + 1, n_chunks - 1), n_chunks - 1), 0))
    y_spec = pl.BlockSpec((CHUNK, D_MODEL),
                          lambda b, l, c: (b * n_chunks + jnp.where(l == last, c, 0), 0))
    in_specs = [
        x0_spec, xnext_spec,
        per_seq((N_MEM, XA_WIDTH)), per_seq((N_MEM, XA_WIDTH)),
        next_lay3((1, D_MODEL)),
        next_lay3((D_MODEL, PROJ_WIDTH)),
        lay3((D_MIX, D_MODEL)),
        pl.BlockSpec((None, 4, POOL_GROUP_DIM, POOL_GROUP_DIM), lambda b, l, c: (l, 0, 0, 0)),
        lay3((1, POOL_WIDTH)),
        lay3((CONV_WIDTH, CONV_DIM)),
        lay3((1, CONV_DIM)),
        lay3((1, LANES)),
        lay3((1, LANES)),
        lay3((1, SSD_WIDTH)),
        lay3((1, SSD_WIDTH)),
        pl.BlockSpec((1, D_MODEL), lambda b, l, c: (0, 0)),
        pl.BlockSpec((LANES, N_EXPAND * SSD_WIDTH), lambda b, l, c: (0, 0)),
    ]
    out_specs = [
        y_spec,
        state_out((POOL_STATE_LEN, POOL_WIDTH)),
        state_out((CONV_WIDTH - 1, CONV_DIM)),
        state_out((SSD_WIDTH, D_STATE)),
    ]
    out_shape = [
        jax.ShapeDtypeStruct((batch * seq_len, D_MODEL), F32),
        jax.ShapeDtypeStruct((DEPTH, batch, POOL_STATE_LEN, POOL_WIDTH), F32),
        jax.ShapeDtypeStruct((DEPTH, batch, CONV_WIDTH - 1, CONV_DIM), F32),
        jax.ShapeDtypeStruct((DEPTH, batch, SSD_WIDTH, D_STATE), F32),
    ]
    scratch = [
        pltpu.VMEM((seq_len, D_MODEL), F32),
        pltpu.VMEM((CHUNK, D_MODEL), BF16),
        pltpu.VMEM((CHUNK, PROJ_WIDTH), F32),
        pltpu.VMEM((CHUNK, PROJ_WIDTH), F32),
        pltpu.VMEM((1, POOL_HIST + CHUNK, POOL_WIDTH), F32),
        pltpu.VMEM((1, POOL_HIST + CHUNK, POOL_WIDTH), F32),
        pltpu.VMEM((1, POOL_HIST + CHUNK, POOL_WIDTH), F32),
        pltpu.VMEM((1, CONV_HIST + CHUNK, CONV_DIM), F32),
        pltpu.VMEM((CHUNK, CONV_DIM), F32),
        pltpu.VMEM((CHUNK, SSD_WIDTH), F32),
        pltpu.VMEM((CHUNK, D_MIX), BF16),
    ]
    return pl.pallas_call(
        functools.partial(_prompt_kernel, n_chunks=n_chunks),
        grid=(batch, DEPTH, n_chunks),
        in_specs=in_specs,
        out_specs=out_specs,
        out_shape=out_shape,
        scratch_shapes=scratch,
        compiler_params=pltpu.CompilerParams(
            dimension_semantics=("arbitrary", "arbitrary", "arbitrary"),
            vmem_limit_bytes=VMEM_LIMIT_BYTES),
        name="prompt_stack",
    )(x2d, x2d, k_dense, v_dense, ng, w_in_r, *lw, fng, sel)


def _norm_proj_kernel(x_ref, g_ref, w_ref, o_ref):
    o_ref[...] = _dot(_rms_scale(x_ref[...], g_ref[...]).astype(BF16), w_ref[...])


def _norm_proj(x, g_all, w_all, layer, tm):
    t, d = x.shape
    n = w_all.shape[-1]
    assert t % tm == 0
    return pl.pallas_call(
        _norm_proj_kernel,
        grid=(t // tm,),
        in_specs=[
            pl.BlockSpec((tm, d), lambda i: (i, 0)),
            pl.BlockSpec((None, 1, d), lambda i: (layer, 0, 0)),
            pl.BlockSpec((None, d, n), lambda i: (layer, 0, 0)),
        ],
        out_specs=pl.BlockSpec((tm, n), lambda i: (i, 0)),
        out_shape=jax.ShapeDtypeStruct((t, n), F32),
        compiler_params=pltpu.CompilerParams(
            dimension_semantics=("arbitrary",), vmem_limit_bytes=VMEM_LIMIT_BYTES),
        name="norm_proj",
    )(x, g_all, w_all)


def _sample_kernel(*refs, nb, seq, n_alias, final):
    (proj_ref, h_ref, k_ref, v_ref, pool0_ref, conv0_ref, ssm0_ref,
     wout_ref, poolw_ref, pscale_ref, convw_ref, convb_ref, dtb_ref, arow_ref, dskip_ref,
     normg_ref, fng_ref, sel_ref) = refs[:18]
    (hout_ref, poolo_ref, convo_ref, ssm_ref,
     extp_ref, pa_ref, pb_ref, extc_ref, xc_ref, y_ref, mix_ref) = refs[18 + n_alias:]

    _init_history(extp_ref, extc_ref, ssm_ref, (pool0_ref, conv0_ref, ssm0_ref))
    _pool_heads(proj_ref, extp_ref, pa_ref, pb_ref, mix_ref, poolw_ref, pscale_ref, nb, seq, PAST_LEN)
    poolo_ref[...] = extp_ref[:, seq + POOL_HIST - POOL_STATE_LEN:seq + POOL_HIST, :]
    _conv_silu(proj_ref, extc_ref, xc_ref, convw_ref, convb_ref, nb, seq)
    convo_ref[...] = extc_ref[:, seq + CONV_HIST - (CONV_WIDTH - 1):seq + CONV_HIST, :]
    _ssd_chunk(proj_ref, xc_ref, y_ref, ssm_ref, dtb_ref, arow_ref, sel_ref, nb, seq)
    _gated_group_norm(proj_ref, xc_ref, y_ref, mix_ref, dskip_ref, normg_ref)
    _xattn_rowmajor(proj_ref, mix_ref, k_ref, v_ref, nb, seq)
    h_new = h_ref[...] + _dot(mix_ref[...], wout_ref[...])
    hout_ref[...] = _rms_scale(h_new, fng_ref[...]) if final else h_new


def _sample_mixer(proj, h, k_all, v_all, st_pool, st_conv, st_ssm, lw, fng, sel, stacked, layer, nb, seq):
    n_seq = h.shape[0] // seq
    t = nb * seq
    assert n_seq % nb == 0
    final = layer == DEPTH - 1

    def per_seq(shape):
        return pl.BlockSpec((None, nb) + shape, lambda i: (layer, i) + (0,) * len(shape))

    def lay(shape):
        return pl.BlockSpec((None,) + shape, lambda i: (layer,) + (0,) * len(shape))

    rows = lambda width: pl.BlockSpec((t, width), lambda i: (i, 0))
    in_specs = [
        rows(PROJ_WIDTH), rows(D_MODEL),
        per_seq((N_MEM * XA_HEADS, XA_HEAD_DIM)), per_seq((N_MEM * XA_HEADS, XA_HEAD_DIM)),
        per_seq((POOL_STATE_LEN, POOL_WIDTH)), per_seq((CONV_WIDTH - 1, CONV_DIM)),
        per_seq((SSD_WIDTH, D_STATE)),
        lay((D_MIX, D_MODEL)), lay((4, POOL_GROUP_DIM, POOL_GROUP_DIM)), lay((1, POOL_WIDTH)),
        lay((CONV_WIDTH, CONV_DIM)), lay((1, CONV_DIM)), lay((1, LANES)), lay((1, LANES)),
        lay((1, SSD_WIDTH)), lay((1, SSD_WIDTH)),
        pl.BlockSpec((1, D_MODEL), lambda i: (0, 0)),
        pl.BlockSpec((LANES, N_EXPAND * SSD_WIDTH), lambda i: (0, 0)),
    ]
    args = [proj, h, k_all, v_all, st_pool, st_conv, st_ssm, *lw, fng, sel]
    aliases = {}
    n_alias = 0
    if stacked is not None:
        n_alias = len(stacked)
        for j, arr in enumerate(stacked):
            aliases[len(args)] = 1 + j
            in_specs.append(pl.BlockSpec(memory_space=pl.ANY))
            args.append(arr)
    out_specs = [
        rows(D_MODEL),
        per_seq((POOL_STATE_LEN, POOL_WIDTH)), per_seq((CONV_WIDTH - 1, CONV_DIM)),
        per_seq((SSD_WIDTH, D_STATE)),
    ]
    out_shape = [
        jax.ShapeDtypeStruct(h.shape, F32),
        jax.ShapeDtypeStruct((DEPTH, n_seq, POOL_STATE_LEN, POOL_WIDTH), F32),
        jax.ShapeDtypeStruct((DEPTH, n_seq, CONV_WIDTH - 1, CONV_DIM), F32),
        jax.ShapeDtypeStruct((DEPTH, n_seq, SSD_WIDTH, D_STATE), F32),
    ]
    scratch = [
        pltpu.VMEM((nb, POOL_HIST + seq, POOL_WIDTH), F32),
        pltpu.VMEM((nb, POOL_HIST + seq, POOL_WIDTH), F32),
        pltpu.VMEM((nb, POOL_HIST + seq, POOL_WIDTH), F32),
        pltpu.VMEM((nb, CONV_HIST + seq, CONV_DIM), F32),
        pltpu.VMEM((t, CONV_DIM), F32),
        pltpu.VMEM((t, SSD_WIDTH), F32),
        pltpu.VMEM((t, D_MIX), BF16),
    ]
    return pl.pallas_call(
        functools.partial(_sample_kernel, nb=nb, seq=seq, n_alias=n_alias, final=final),
        grid=(n_seq // nb,),
        in_specs=in_specs,
        out_specs=out_specs,
        out_shape=out_shape,
        scratch_shapes=scratch,
        input_output_aliases=aliases,
        compiler_params=pltpu.CompilerParams(
            dimension_semantics=("arbitrary",), vmem_limit_bytes=VMEM_LIMIT_BYTES),
        name="sample_mixer",
    )(*args)


def _reorder_w_in(w_in):
    o = 0
    parts = {}
    for name, width in (("u", POOL_WIDTH), ("gp", POOL_WIDTH), ("z", SSD_WIDTH), ("xbc", CONV_DIM),
                        ("dt", SSD_HEADS), ("q", XA_WIDTH), ("gx", XA_WIDTH)):
        parts[name] = w_in[..., o:o + width]
        o += width
    dt_rep = jnp.concatenate([parts["dt"]] * DT_COPIES, axis=-1)
    return jnp.concatenate(
        [parts["u"], parts["gp"], parts["z"], parts["xbc"], parts["q"], parts["gx"], dt_rep],
        axis=-1).astype(BF16)


def _expand_selector():
    lane = jnp.arange(LANES, dtype=jnp.int32)
    quantity = jnp.asarray(EXPAND_GROUPS, jnp.int32)[lane // SSD_HEADS]
    head = lane % SSD_HEADS
    col = jnp.arange(N_EXPAND * SSD_WIDTH, dtype=jnp.int32)
    col_quantity = col // SSD_WIDTH
    col_head = (col % SSD_WIDTH) // SSD_HEAD_DIM
    return jnp.logical_and(quantity[:, None] == col_quantity[None, :],
                           head[:, None] == col_head[None, :]).astype(BF16)


def kernel(x_prompt, x_sample, mem_prompt, state_pool, state_conv, state_ssm, cache_mem_k, cache_mem_v,
           norm_g, w_in, pool_w, pool_scale, conv_w, conv_b, dt_bias, a_log, d_skip, ssd_norm_g,
           mem_norm_g, w_mem_k, w_mem_v, w_out, final_norm_g):
    batch, seq_p, _ = x_prompt.shape
    dec_batch, seq_s, _ = x_sample.shape

    w_in_r = _reorder_w_in(w_in)
    w_kv = jnp.concatenate([w_mem_k, w_mem_v], axis=-1).astype(BF16)
    sel = _expand_selector()
    fng = final_norm_g.reshape(1, D_MODEL)
    ng = norm_g.reshape(DEPTH, 1, D_MODEL)
    lw = (w_out.astype(BF16), pool_w.astype(BF16), pool_scale.reshape(DEPTH, 1, POOL_WIDTH), conv_w,
          conv_b.reshape(DEPTH, 1, CONV_DIM),
          jnp.tile(dt_bias, (1, DT_COPIES)).reshape(DEPTH, 1, LANES),
          jnp.tile(-jnp.exp(a_log), (1, DT_COPIES)).reshape(DEPTH, 1, LANES),
          jnp.repeat(d_skip, SSD_HEAD_DIM, axis=-1).reshape(DEPTH, 1, SSD_WIDTH),
          ssd_norm_g.reshape(DEPTH, 1, SSD_WIDTH))

    k_flat, v_flat, k_dense, v_dense = _mem_kv(
        mem_prompt.reshape(batch * N_MEM, D_MODEL), mem_norm_g, w_kv)
    y_p, pool_p, conv_p, ssm_p = _prompt_stack(
        x_prompt.reshape(batch * seq_p, D_MODEL), k_dense, v_dense, ng, w_in_r, lw, fng, sel,
        batch, seq_p)

    hs = x_sample.reshape(dec_batch * seq_s, D_MODEL)
    st_ssm = state_ssm.reshape(DEPTH, dec_batch, SSD_WIDTH, D_STATE)
    ck = cache_mem_k.reshape(DEPTH, dec_batch, N_MEM * XA_HEADS, XA_HEAD_DIM)
    cv = cache_mem_v.reshape(DEPTH, dec_batch, N_MEM * XA_HEADS, XA_HEAD_DIM)
    stacked = None
    for layer in range(DEPTH):
        proj_s = _norm_proj(hs, ng, w_in_r, layer, 512)
        hs, *stacked = _sample_mixer(proj_s, hs, ck, cv, state_pool, state_conv, st_ssm, lw, fng, sel,
                                     stacked, layer, 8, seq_s)
    pool_s, conv_s, ssm_s = stacked

    mem_shape = (DEPTH, batch, N_MEM, XA_HEADS, XA_HEAD_DIM)
    return (y_p.reshape(batch, seq_p, D_MODEL), hs.reshape(dec_batch, seq_s, D_MODEL),
            pool_p, conv_p, ssm_p.reshape(DEPTH, batch, SSD_HEADS, SSD_HEAD_DIM, D_STATE),
            k_flat.reshape(mem_shape), v_flat.reshape(mem_shape),
            pool_s, conv_s, ssm_s.reshape(DEPTH, dec_batch, SSD_HEADS, SSD_HEAD_DIM, D_STATE))
```

```python
import functools
import math

import jax
import jax.numpy as jnp
from jax import lax
from jax.experimental import pallas as pl
from jax.experimental.pallas import tpu as pltpu

F32 = jnp.float32
BF16 = jnp.bfloat16

D_MODEL = 1024
DEPTH = 4
PAST_LEN = 16384
POOL_WIDTH = 512
POOL_WINDOWS = (2, 4, 8, 16)
POOL_GROUP_DIM = 128
POOL_STATE_LEN = 15
SSD_WIDTH = 1024
SSD_HEAD_DIM = 64
SSD_HEADS = 16
SSD_GROUPS = 2
SSD_HEADS_PER_GROUP = 8
SSD_GROUP_WIDTH = SSD_WIDTH // SSD_GROUPS
D_STATE = 128
CONV_WIDTH = 4
CONV_DIM = 1536
XA_WIDTH = 512
XA_HEADS = 4
XA_HEAD_DIM = 128
N_MEM = 256
D_MIX = 2048
EPS = 1e-6
LOG2E = 1.4426950408889634

LANES = 128
SUBLANES = 8
POOL_HIST = 32
CONV_HIST = 8
CHUNK = 128
PROMPT_STEP = 256

COL_U = 0
COL_GP = COL_U + POOL_WIDTH
COL_Z = COL_GP + POOL_WIDTH
COL_XBC = COL_Z + SSD_WIDTH
COL_Q = COL_XBC + CONV_DIM
COL_GX = COL_Q + XA_WIDTH
COL_DT = COL_GX + XA_WIDTH
DT_COPIES = LANES // SSD_HEADS
PROJ_WIDTH = COL_DT + LANES

EXPAND_GROUPS = (0, 0, 1, 1, 2, 2, 0, 1)
EXPAND_TERM = (0, 1, 0, 1, 0, 1, 2, 2)
N_EXPAND = 3

VMEM_LIMIT_BYTES = 56 * 1024 * 1024


def _silu(x):
    h = 0.5 * x
    return h + h * jnp.tanh(h)


def _softplus(x):
    return jnp.maximum(x, 0.0) + jnp.log1p(jnp.exp(-jnp.abs(x)))


def _bf16_terms(v, n):
    terms = []
    for _ in range(n):
        t = v.astype(BF16).astype(F32)
        terms.append(t)
        v = v - t
    return terms


def _dot(a, b):
    return jnp.dot(a, b, preferred_element_type=F32)


def _dot_nt(a, b):
    return lax.dot_general(a, b, (((1,), (1,)), ((), ())), preferred_element_type=F32)


def _dot_tn(a, b):
    return lax.dot_general(a, b, (((0,), (0,)), ((), ())), preferred_element_type=F32)


def _exact_left_dot(sel_bf16, v):
    return sum(_dot(sel_bf16, t.astype(BF16)) for t in _bf16_terms(v, 3))


def _rms_scale(x, g):
    return x * lax.rsqrt(jnp.mean(x * x, axis=-1, keepdims=True) + EPS) * g


def _init_history(extp_ref, extc_ref, ssm_ref, init_refs):
    nb = extp_ref.shape[0]
    extp_ref[:, 0:POOL_HIST, :] = jnp.zeros((nb, POOL_HIST, POOL_WIDTH), F32)
    if init_refs is None:
        extc_ref[:, 0:CONV_HIST, :] = jnp.zeros((nb, CONV_HIST, CONV_DIM), F32)
        ssm_ref[...] = jnp.zeros(ssm_ref.shape, F32)
    else:
        pool0_ref, conv0_ref, ssm0_ref = init_refs
        extp_ref[:, POOL_HIST - POOL_STATE_LEN:POOL_HIST, :] = pool0_ref[...]
        extc_ref[:, CONV_HIST - (CONV_WIDTH - 1):CONV_HIST, :] = conv0_ref[...]
        ssm_ref[...] = ssm0_ref[...]


def _pool_heads(proj_ref, extp_ref, pa_ref, pb_ref, mix_ref, poolw_ref, pscale_ref, nb, seq, pos0):
    t = nb * seq
    n = POOL_HIST + seq
    extp_ref[:, POOL_HIST:n, :] = proj_ref[:, COL_U:COL_U + POOL_WIDTH].reshape(nb, seq, POOL_WIDTH)
    g = POOL_GROUP_DIM
    pa_ref[:, 8:n, :] = extp_ref[:, 8:n, :] + extp_ref[:, 7:n - 1, :]
    pb_ref[:, 16:n, g:4 * g] = pa_ref[:, 16:n, g:4 * g] + pa_ref[:, 14:n - 2, g:4 * g]
    pa_ref[:, 24:n, 2 * g:4 * g] = pb_ref[:, 24:n, 2 * g:4 * g] + pb_ref[:, 20:n - 4, 2 * g:4 * g]
    pb_ref[:, 32:n, 3 * g:4 * g] = pa_ref[:, 32:n, 3 * g:4 * g] + pa_ref[:, 24:n - 8, 3 * g:4 * g]
    tok = lax.broadcasted_iota(jnp.int32, (nb, seq, LANES), 1)
    pos1 = pos0 + tok + 1
    for gi, w in enumerate(POOL_WINDOWS):
        sl = slice(gi * g, (gi + 1) * g)
        src = pa_ref if gi % 2 == 0 else pb_ref
        cnt = jnp.minimum(pos1, w).astype(F32)
        d = (src[:, POOL_HIST:n, sl] / cnt - extp_ref[:, POOL_HIST:n, sl]).reshape(t, g)
        o = _dot(d.astype(BF16), poolw_ref[gi]) * pscale_ref[:, sl]
        gp = proj_ref[:, COL_GP + gi * g:COL_GP + (gi + 1) * g]
        mix_ref[:, sl] = (o * _silu(gp)).astype(BF16)


def _conv_silu(proj_ref, extc_ref, xc_ref, convw_ref, convb_ref, nb, seq):
    t = nb * seq
    extc_ref[:, CONV_HIST:CONV_HIST + seq, :] = proj_ref[:, COL_XBC:COL_XBC + CONV_DIM].reshape(
        nb, seq, CONV_DIM)
    cblk = 256
    for j in range(CONV_DIM // cblk):
        sl = slice(j * cblk, (j + 1) * cblk)
        if nb == 1:
            ext = extc_ref[0, :, sl]
            acc = convw_ref[0:1, sl] * ext
            for k in range(1, CONV_WIDTH):
                acc = pltpu.roll(acc, 1, 0) + convw_ref[k:k + 1, sl] * ext
            acc = acc[CONV_HIST:CONV_HIST + seq] + convb_ref[:, sl]
        else:
            acc = convb_ref[:, sl][None]
            for k in range(CONV_WIDTH):
                r0 = CONV_HIST - (CONV_WIDTH - 1) + k
                acc = acc + convw_ref[k:k + 1, sl][None] * extc_ref[:, r0:r0 + seq, sl]
        xc_ref[:, sl] = _silu(acc).reshape(t, cblk)


def _expand_heads(vals, sel_ref):
    t = vals[0].shape[0]
    terms = [_bf16_terms(v, 1 + max(tm for gq, tm in zip(EXPAND_GROUPS, EXPAND_TERM) if gq == i))
             for i, v in enumerate(vals)]
    group = lax.shift_right_logical(lax.broadcasted_iota(jnp.int32, (t, LANES), 1), 4)
    packed = terms[EXPAND_GROUPS[0]][EXPAND_TERM[0]]
    for gidx in range(1, len(EXPAND_GROUPS)):
        packed = jnp.where(group == gidx, terms[EXPAND_GROUPS[gidx]][EXPAND_TERM[gidx]], packed)
    out = _dot(packed.astype(BF16), sel_ref[...])
    return [out[:, i * SSD_WIDTH:(i + 1) * SSD_WIDTH] for i in range(len(vals))]


def _ssd_chunk(proj_ref, xc_ref, y_ref, ssm_ref, dtb_ref, arow_ref, sel_ref, nb, seq):
    t = nb * seq
    dt = _softplus(proj_ref[:, COL_DT:COL_DT + LANES] + dtb_ref[...])
    a = dt * arow_ref[...]
    ri = lax.broadcasted_iota(jnp.int32, (t, t), 0)
    ci = lax.broadcasted_iota(jnp.int32, (t, t), 1)
    if nb > 1:
        seq_start = ri - lax.rem(ri, seq)
        causal = jnp.bitwise_or(ci - seq_start, ri - ci) >= 0
    else:
        causal = ci <= ri
    tri = jnp.where(causal, 1.0, 0.0).astype(BF16)
    cum = _exact_left_dot(tri, a)
    cum2 = cum * LOG2E
    pad = LANES - t
    if pad:
        cum2_t = jnp.concatenate([cum2, jnp.zeros((pad, LANES), F32)], axis=0).T[:, 0:t]
    else:
        cum2_t = cum2.T
    cum3 = cum.reshape(nb, seq, LANES)
    cum_last = cum3[:, seq - 1:seq, :]
    e_tok = jnp.exp(cum)
    w_tok = (jnp.exp(cum_last - cum3) * dt.reshape(nb, seq, LANES)).reshape(t, LANES)
    dec = jnp.exp(cum_last)
    e_exp, w_exp, dt_exp = _expand_heads([e_tok, w_tok, dt], sel_ref)

    lane = lax.broadcasted_iota(jnp.int32, (t, LANES), 1)
    low_half = lane < SSD_HEAD_DIM

    for g in range(SSD_GROUPS):
        b0 = SSD_WIDTH + g * D_STATE
        c0 = SSD_WIDTH + SSD_GROUPS * D_STATE + g * D_STATE
        bg = xc_ref[:, b0:b0 + D_STATE].astype(BF16)
        cg = xc_ref[:, c0:c0 + D_STATE].astype(BF16)
        cb = jnp.where(causal, _dot_nt(cg, bg), 0.0)
        for pp in range(SSD_HEADS_PER_GROUP // 2):
            h0 = g * SSD_HEADS_PER_GROUP + 2 * pp
            ms = []
            for hh in (h0, h0 + 1):
                seg2 = cum2[:, hh:hh + 1] - cum2_t[hh:hh + 1, :]
                ms.append((cb * jnp.exp2(jnp.minimum(seg2, 0.0))).astype(BF16))
            col = (h0 // 2) * LANES
            xdt = xc_ref[:, col:col + LANES] * dt_exp[:, col:col + LANES]
            rhs = jnp.concatenate(
                [jnp.where(low_half, xdt, 0.0), jnp.where(low_half, 0.0, xdt)], axis=0).astype(BF16)
            y_ref[:, col:col + LANES] = _dot(jnp.concatenate(ms, axis=1), rhs)
        gsl = slice(g * SSD_GROUP_WIDTH, (g + 1) * SSD_GROUP_WIDTH)
        for b in range(nb):
            rows = slice(b * seq, (b + 1) * seq)
            s_old = ssm_ref[b, gsl, :]
            yi = _dot_nt(cg[rows], s_old.astype(BF16))
            y_ref[rows, gsl] = y_ref[rows, gsl] + yi * e_exp[rows, gsl]
            xw = (xc_ref[rows, gsl] * w_exp[rows, gsl]).astype(BF16)
            upd = _dot_tn(xw, bg[rows])
            for r in range(SSD_HEADS_PER_GROUP):
                hh = g * SSD_HEADS_PER_GROUP + r
                hs = slice(r * SSD_HEAD_DIM, (r + 1) * SSD_HEAD_DIM)
                r0 = g * SSD_GROUP_WIDTH + r * SSD_HEAD_DIM
                ssm_ref[b, r0:r0 + SSD_HEAD_DIM, :] = s_old[hs] * dec[b, :, hh:hh + 1] + upd[hs]


def _gated_group_norm(proj_ref, xc_ref, y_ref, mix_ref, dskip_ref, normg_ref):
    z = proj_ref[:, COL_Z:COL_Z + SSD_WIDTH]
    y = (y_ref[...] + dskip_ref[...] * xc_ref[:, 0:SSD_WIDTH]) * _silu(z)
    for g in range(SSD_GROUPS):
        gsl = slice(g * SSD_GROUP_WIDTH, (g + 1) * SSD_GROUP_WIDTH)
        mix_ref[:, POOL_WIDTH + g * SSD_GROUP_WIDTH:POOL_WIDTH + (g + 1) * SSD_GROUP_WIDTH] = (
            _rms_scale(y[:, gsl], normg_ref[:, gsl])).astype(BF16)


def _softmax_rows(s):
    s = s - jnp.max(s, axis=-1, keepdims=True)
    p = jnp.exp(s)
    return p / jnp.sum(p, axis=-1, keepdims=True)


def _xattn_dense(proj_ref, mix_ref, k_ref, v_ref):
    inv_sqrt_d = 1.0 / math.sqrt(XA_HEAD_DIM)
    for hh in range(XA_HEADS):
        sl = slice(hh * XA_HEAD_DIM, (hh + 1) * XA_HEAD_DIM)
        q = proj_ref[:, COL_Q + hh * XA_HEAD_DIM:COL_Q + (hh + 1) * XA_HEAD_DIM].astype(BF16)
        p = _softmax_rows(_dot_nt(q, k_ref[:, sl]) * inv_sqrt_d)
        o = _dot(p.astype(BF16), v_ref[:, sl])
        gx = proj_ref[:, COL_GX + hh * XA_HEAD_DIM:COL_GX + (hh + 1) * XA_HEAD_DIM]
        m0 = POOL_WIDTH + SSD_WIDTH + hh * XA_HEAD_DIM
        mix_ref[:, m0:m0 + XA_HEAD_DIM] = (o * _silu(gx)).astype(BF16)


def _xattn_rowmajor(proj_ref, mix_ref, k_ref, v_ref, nb, seq):
    t = nb * seq
    inv_sqrt_d = 1.0 / math.sqrt(XA_HEAD_DIM)
    q4 = jnp.concatenate(
        [proj_ref[:, COL_Q + hh * XA_HEAD_DIM:COL_Q + (hh + 1) * XA_HEAD_DIM].astype(BF16).reshape(
            nb, seq, XA_HEAD_DIM) for hh in range(XA_HEADS)], axis=1)
    kf = k_ref[...].astype(BF16)
    vf = v_ref[...].astype(BF16)
    s = lax.dot_general(q4, kf, (((2,), (2,)), ((0,), (0,))), preferred_element_type=F32) * inv_sqrt_d
    shape = (nb, XA_HEADS * seq, N_MEM * XA_HEADS)
    row = lax.broadcasted_iota(jnp.int32, shape, 1)
    col = lax.broadcasted_iota(jnp.int32, shape, 2)
    own_head = jnp.bitwise_and(col, XA_HEADS - 1) * seq
    valid = jnp.bitwise_or(row - own_head, own_head + (seq - 1) - row) >= 0
    p = _softmax_rows(jnp.where(valid, s, -1e30))
    o4 = lax.dot_general(p.astype(BF16), vf, (((2,), (1,)), ((0,), (0,))), preferred_element_type=F32)
    for hh in range(XA_HEADS):
        o = o4[:, hh * seq:(hh + 1) * seq, :].reshape(t, XA_HEAD_DIM)
        gx = proj_ref[:, COL_GX + hh * XA_HEAD_DIM:COL_GX + (hh + 1) * XA_HEAD_DIM]
        m0 = POOL_WIDTH + SSD_WIDTH + hh * XA_HEAD_DIM
        mix_ref[:, m0:m0 + XA_HEAD_DIM] = (o * _silu(gx)).astype(BF16)


def _mem_kv_kernel(mem_ref, g_ref, w_ref, kf_ref, vf_ref, kd_ref, vd_ref):
    xn = _rms_scale(mem_ref[...], g_ref[...]).astype(BF16)
    kv = _dot(xn, w_ref[...])
    kd_ref[...] = kv[:, 0:XA_WIDTH].astype(BF16)
    vd_ref[...] = kv[:, XA_WIDTH:2 * XA_WIDTH].astype(BF16)
    for hh in range(XA_HEADS):
        kf_ref[pl.ds(hh, N_MEM, stride=XA_HEADS), :] = kv[:, hh * XA_HEAD_DIM:(hh + 1) * XA_HEAD_DIM]
        vf_ref[pl.ds(hh, N_MEM, stride=XA_HEADS), :] = kv[
            :, XA_WIDTH + hh * XA_HEAD_DIM:XA_WIDTH + (hh + 1) * XA_HEAD_DIM]


def _mem_kv(mem2d, mem_norm_g, w_kv):
    batch = mem2d.shape[0] // N_MEM
    flat = jax.ShapeDtypeStruct((DEPTH, batch, N_MEM * XA_HEADS, XA_HEAD_DIM), F32)
    dense = jax.ShapeDtypeStruct((DEPTH, batch, N_MEM, XA_WIDTH), BF16)
    flat_spec = pl.BlockSpec((None, None, N_MEM * XA_HEADS, XA_HEAD_DIM), lambda l, b: (l, b, 0, 0))
    dense_spec = pl.BlockSpec((None, None, N_MEM, XA_WIDTH), lambda l, b: (l, b, 0, 0))
    return pl.pallas_call(
        _mem_kv_kernel,
        grid=(DEPTH, batch),
        in_specs=[
            pl.BlockSpec((N_MEM, D_MODEL), lambda l, b: (b, 0)),
            pl.BlockSpec((None, 1, D_MODEL), lambda l, b: (l, 0, 0)),
            pl.BlockSpec((None, D_MODEL, 2 * XA_WIDTH), lambda l, b: (l, 0, 0)),
        ],
        out_specs=[flat_spec, flat_spec, dense_spec, dense_spec],
        out_shape=[flat, flat, dense, dense],
        compiler_params=pltpu.CompilerParams(dimension_semantics=("arbitrary", "arbitrary")),
        name="mem_kv",
    )(mem2d, mem_norm_g.reshape(DEPTH, 1, D_MODEL), w_kv)


def _prompt_kernel(x_ref, k_ref, v_ref, ng_ref, win_ref, wout_ref, poolw_ref, pscale_ref, convw_ref,
                   convb_ref, dtb_ref, arow_ref, dskip_ref, normg_ref, fng_ref, sel_ref,
                   y_out_ref, poolo_ref, convo_ref, ssm_ref,
                   h_ref, proj_ref, extp_ref, pa_ref, pb_ref, extc_ref, xc_ref, y_ref, mix_ref):
    layer = pl.program_id(1)
    c = pl.program_id(2)
    rows = pl.ds(pl.multiple_of(c * PROMPT_STEP, PROMPT_STEP), PROMPT_STEP)

    @pl.when(layer == 0)
    def _load():
        h_ref[rows, :] = x_ref[...]

    @pl.when(c == 0)
    def _init():
        _init_history(extp_ref, extc_ref, ssm_ref, None)

    h = h_ref[rows, :]
    proj_ref[...] = _dot(_rms_scale(h, ng_ref[...]).astype(BF16), win_ref[...])

    _pool_heads(proj_ref, extp_ref, pa_ref, pb_ref, mix_ref, poolw_ref, pscale_ref, 1, PROMPT_STEP,
                c * PROMPT_STEP)
    poolo_ref[...] = extp_ref[:, PROMPT_STEP + POOL_HIST - POOL_STATE_LEN:PROMPT_STEP + POOL_HIST, :]
    extp_ref[:, 0:POOL_HIST, :] = extp_ref[:, PROMPT_STEP:PROMPT_STEP + POOL_HIST, :]

    _conv_silu(proj_ref, extc_ref, xc_ref, convw_ref, convb_ref, 1, PROMPT_STEP)
    convo_ref[...] = extc_ref[:, PROMPT_STEP + CONV_HIST - (CONV_WIDTH - 1):PROMPT_STEP + CONV_HIST, :]
    extc_ref[:, CONV_HIST - (CONV_WIDTH - 1):CONV_HIST, :] = extc_ref[
        :, PROMPT_STEP + CONV_HIST - (CONV_WIDTH - 1):PROMPT_STEP + CONV_HIST, :]

    for s in range(PROMPT_STEP // CHUNK):
        sub = pl.ds(s * CHUNK, CHUNK)
        _ssd_chunk(proj_ref.at[sub], xc_ref.at[sub], y_ref.at[sub], ssm_ref, dtb_ref, arow_ref,
                   sel_ref, 1, CHUNK)
    _gated_group_norm(proj_ref, xc_ref, y_ref, mix_ref, dskip_ref, normg_ref)
    _xattn_dense(proj_ref, mix_ref, k_ref, v_ref)

    h_new = h + _dot(mix_ref[...], wout_ref[...])
    h_ref[rows, :] = h_new

    @pl.when(layer == DEPTH - 1)
    def _final():
        y_out_ref[...] = _rms_scale(h_new, fng_ref[...])


def _prompt_stack(x2d, k_dense, v_dense, ng, w_in_r, lw, fng, sel, batch, seq_len):
    n_steps = seq_len // PROMPT_STEP
    assert n_steps * PROMPT_STEP == seq_len
    last = DEPTH - 1

    def lay3(shape):
        return pl.BlockSpec((None,) + shape, lambda b, l, c: (l, 0, 0))

    def per_seq(shape):
        return pl.BlockSpec((None, None) + shape, lambda b, l, c: (l, b, 0, 0))

    def state_out(shape):
        return pl.BlockSpec((None, 1) + shape, lambda b, l, c: (l, b, 0, 0))

    x_spec = pl.BlockSpec((PROMPT_STEP, D_MODEL),
                          lambda b, l, c: (b * n_steps + jnp.where(l == 0, c, n_steps - 1), 0))
    y_spec = pl.BlockSpec((PROMPT_STEP, D_MODEL),
                          lambda b, l, c: (b * n_steps + jnp.where(l == last, c, 0), 0))
    in_specs = [
        x_spec,
        per_seq((N_MEM, XA_WIDTH)), per_seq((N_MEM, XA_WIDTH)),
        lay3((1, D_MODEL)),
        lay3((D_MODEL, PROJ_WIDTH)),
        lay3((D_MIX, D_MODEL)),
        pl.BlockSpec((None, 4, POOL_GROUP_DIM, POOL_GROUP_DIM), lambda b, l, c: (l, 0, 0, 0)),
        lay3((1, POOL_WIDTH)),
        lay3((CONV_WIDTH, CONV_DIM)),
        lay3((1, CONV_DIM)),
        lay3((1, LANES)),
        lay3((1, LANES)),
        lay3((1, SSD_WIDTH)),
        lay3((1, SSD_WIDTH)),
        pl.BlockSpec((1, D_MODEL), lambda b, l, c: (0, 0)),
        pl.BlockSpec((LANES, N_EXPAND * SSD_WIDTH), lambda b, l, c: (0, 0)),
    ]
    out_specs = [
        y_spec,
        state_out((POOL_STATE_LEN, POOL_WIDTH)),
        state_out((CONV_WIDTH - 1, CONV_DIM)),
        state_out((SSD_WIDTH, D_STATE)),
    ]
    out_shape = [
        jax.ShapeDtypeStruct((batch * seq_len, D_MODEL), F32),
        jax.ShapeDtypeStruct((DEPTH, batch, POOL_STATE_LEN, POOL_WIDTH), F32),
        jax.ShapeDtypeStruct((DEPTH, batch, CONV_WIDTH - 1, CONV_DIM), F32),
        jax.ShapeDtypeStruct((DEPTH, batch, SSD_WIDTH, D_STATE), F32),
    ]
    scratch = [
        pltpu.VMEM((seq_len, D_MODEL), F32),
        pltpu.VMEM((PROMPT_STEP, PROJ_WIDTH), F32),
        pltpu.VMEM((1, POOL_HIST + PROMPT_STEP, POOL_WIDTH), F32),
        pltpu.VMEM((1, POOL_HIST + PROMPT_STEP, POOL_WIDTH), F32),
        pltpu.VMEM((1, POOL_HIST + PROMPT_STEP, POOL_WIDTH), F32),
        pltpu.VMEM((1, CONV_HIST + PROMPT_STEP, CONV_DIM), F32),
        pltpu.VMEM((PROMPT_STEP, CONV_DIM), F32),
        pltpu.VMEM((PROMPT_STEP, SSD_WIDTH), F32),
        pltpu.VMEM((PROMPT_STEP, D_MIX), BF16),
    ]
    return pl.pallas_call(
        _prompt_kernel,
        grid=(batch, DEPTH, n_steps),
        in_specs=in_specs,
        out_specs=out_specs,
        out_shape=out_shape,
        scratch_shapes=scratch,
        compiler_params=pltpu.CompilerParams(
            dimension_semantics=("arbitrary", "arbitrary", "arbitrary"),
            vmem_limit_bytes=VMEM_LIMIT_BYTES),
        name="prompt_stack",
    )(x2d, k_dense, v_dense, ng, w_in_r, *lw, fng, sel)


def _norm_proj_kernel(x_ref, g_ref, w_ref, o_ref):
    o_ref[...] = _dot(_rms_scale(x_ref[...], g_ref[...]).astype(BF16), w_ref[...])


def _norm_proj(x, g_all, w_all, layer, tm):
    t, d = x.shape
    n = w_all.shape[-1]
    assert t % tm == 0
    return pl.pallas_call(
        _norm_proj_kernel,
        grid=(t // tm,),
        in_specs=[
            pl.BlockSpec((tm, d), lambda i: (i, 0)),
            pl.BlockSpec((None, 1, d), lambda i: (layer, 0, 0)),
            pl.BlockSpec((None, d, n), lambda i: (layer, 0, 0)),
        ],
        out_specs=pl.BlockSpec((tm, n), lambda i: (i, 0)),
        out_shape=jax.ShapeDtypeStruct((t, n), F32),
        compiler_params=pltpu.CompilerParams(
            dimension_semantics=("arbitrary",), vmem_limit_bytes=VMEM_LIMIT_BYTES),
        name="norm_proj",
    )(x, g_all, w_all)


def _sample_kernel(*refs, nb, seq, n_alias, final):
    (proj_ref, h_ref, k_ref, v_ref, pool0_ref, conv0_ref, ssm0_ref,
     wout_ref, poolw_ref, pscale_ref, convw_ref, convb_ref, dtb_ref, arow_ref, dskip_ref,
     normg_ref, fng_ref, sel_ref) = refs[:18]
    (hout_ref, poolo_ref, convo_ref, ssm_ref,
     extp_ref, pa_ref, pb_ref, extc_ref, xc_ref, y_ref, mix_ref) = refs[18 + n_alias:]

    _init_history(extp_ref, extc_ref, ssm_ref, (pool0_ref, conv0_ref, ssm0_ref))
    _pool_heads(proj_ref, extp_ref, pa_ref, pb_ref, mix_ref, poolw_ref, pscale_ref, nb, seq, PAST_LEN)
    poolo_ref[...] = extp_ref[:, seq + POOL_HIST - POOL_STATE_LEN:seq + POOL_HIST, :]
    _conv_silu(proj_ref, extc_ref, xc_ref, convw_ref, convb_ref, nb, seq)
    convo_ref[...] = extc_ref[:, seq + CONV_HIST - (CONV_WIDTH - 1):seq + CONV_HIST, :]
    _ssd_chunk(proj_ref, xc_ref, y_ref, ssm_ref, dtb_ref, arow_ref, sel_ref, nb, seq)
    _gated_group_norm(proj_ref, xc_ref, y_ref, mix_ref, dskip_ref, normg_ref)
    _xattn_rowmajor(proj_ref, mix_ref, k_ref, v_ref, nb, seq)
    h_new = h_ref[...] + _dot(mix_ref[...], wout_ref[...])
    hout_ref[...] = _rms_scale(h_new, fng_ref[...]) if final else h_new


def _sample_mixer(proj, h, k_all, v_all, st_pool, st_conv, st_ssm, lw, fng, sel, stacked, layer, nb, seq):
    n_seq = h.shape[0] // seq
    t = nb * seq
    assert n_seq % nb == 0
    final = layer == DEPTH - 1

    def per_seq(shape):
        return pl.BlockSpec((None, nb) + shape, lambda i: (layer, i) + (0,) * len(shape))

    def lay(shape):
        return pl.BlockSpec((None,) + shape, lambda i: (layer,) + (0,) * len(shape))

    rows = lambda width: pl.BlockSpec((t, width), lambda i: (i, 0))
    in_specs = [
        rows(PROJ_WIDTH), rows(D_MODEL),
        per_seq((N_MEM * XA_HEADS, XA_HEAD_DIM)), per_seq((N_MEM * XA_HEADS, XA_HEAD_DIM)),
        per_seq((POOL_STATE_LEN, POOL_WIDTH)), per_seq((CONV_WIDTH - 1, CONV_DIM)),
        per_seq((SSD_WIDTH, D_STATE)),
        lay((D_MIX, D_MODEL)), lay((4, POOL_GROUP_DIM, POOL_GROUP_DIM)), lay((1, POOL_WIDTH)),
        lay((CONV_WIDTH, CONV_DIM)), lay((1, CONV_DIM)), lay((1, LANES)), lay((1, LANES)),
        lay((1, SSD_WIDTH)), lay((1, SSD_WIDTH)),
        pl.BlockSpec((1, D_MODEL), lambda i: (0, 0)),
        pl.BlockSpec((LANES, N_EXPAND * SSD_WIDTH), lambda i: (0, 0)),
    ]
    args = [proj, h, k_all, v_all, st_pool, st_conv, st_ssm, *lw, fng, sel]
    aliases = {}
    n_alias = 0
    if stacked is not None:
        n_alias = len(stacked)
        for j, arr in enumerate(stacked):
            aliases[len(args)] = 1 + j
            in_specs.append(pl.BlockSpec(memory_space=pl.ANY))
            args.append(arr)
    out_specs = [
        rows(D_MODEL),
        per_seq((POOL_STATE_LEN, POOL_WIDTH)), per_seq((CONV_WIDTH - 1, CONV_DIM)),
        per_seq((SSD_WIDTH, D_STATE)),
    ]
    out_shape = [
        jax.ShapeDtypeStruct(h.shape, F32),
        jax.ShapeDtypeStruct((DEPTH, n_seq, POOL_STATE_LEN, POOL_WIDTH), F32),
        jax.ShapeDtypeStruct((DEPTH, n_seq, CONV_WIDTH - 1, CONV_DIM), F32),
        jax.ShapeDtypeStruct((DEPTH, n_seq, SSD_WIDTH, D_STATE), F32),
    ]
    scratch = [
        pltpu.VMEM((nb, POOL_HIST + seq, POOL_WIDTH), F32),
        pltpu.VMEM((nb, POOL_HIST + seq, POOL_WIDTH), F32),
        pltpu.VMEM((nb, POOL_HIST + seq, POOL_WIDTH), F32),
        pltpu.VMEM((nb, CONV_HIST + seq, CONV_DIM), F32),
        pltpu.VMEM((t, CONV_DIM), F32),
        pltpu.VMEM((t, SSD_WIDTH), F32),
        pltpu.VMEM((t, D_MIX), BF16),
    ]
    return pl.pallas_call(
        functools.partial(_sample_kernel, nb=nb, seq=seq, n_alias=n_alias, final=final),
        grid=(n_seq // nb,),
        in_specs=in_specs,
        out_specs=out_specs,
        out_shape=out_shape,
        scratch_shapes=scratch,
        input_output_aliases=aliases,
        compiler_params=pltpu.CompilerParams(
            dimension_semantics=("arbitrary",), vmem_limit_bytes=VMEM_LIMIT_BYTES),
        name="sample_mixer",
    )(*args)


def _reorder_w_in_kernel(w_ref, o_ref):
    src_dt = COL_Q
    src_q = src_dt + SSD_HEADS
    o_ref[:, 0:COL_Q] = w_ref[:, 0:COL_Q].astype(BF16)
    o_ref[:, COL_Q:COL_DT] = w_ref[:, src_q:src_q + 2 * XA_WIDTH].astype(BF16)
    dt = w_ref[:, src_dt:src_dt + SSD_HEADS].astype(BF16)
    o_ref[:, COL_DT:PROJ_WIDTH] = jnp.concatenate([dt] * DT_COPIES, axis=-1)


def _reorder_w_in(w_in):
    depth, d, n = w_in.shape
    tr = 256
    return pl.pallas_call(
        _reorder_w_in_kernel,
        grid=(depth, d // tr),
        in_specs=[pl.BlockSpec((None, tr, n), lambda l, i: (l, i, 0))],
        out_specs=pl.BlockSpec((None, tr, PROJ_WIDTH), lambda l, i: (l, i, 0)),
        out_shape=jax.ShapeDtypeStruct((depth, d, PROJ_WIDTH), BF16),
        compiler_params=pltpu.CompilerParams(dimension_semantics=("arbitrary", "arbitrary")),
        name="reorder_w_in",
    )(w_in)


def _expand_selector():
    lane = jnp.arange(LANES, dtype=jnp.int32)
    quantity = jnp.asarray(EXPAND_GROUPS, jnp.int32)[lane // SSD_HEADS]
    head = lane % SSD_HEADS
    col = jnp.arange(N_EXPAND * SSD_WIDTH, dtype=jnp.int32)
    col_quantity = col // SSD_WIDTH
    col_head = (col % SSD_WIDTH) // SSD_HEAD_DIM
    return jnp.logical_and(quantity[:, None] == col_quantity[None, :],
                           head[:, None] == col_head[None, :]).astype(BF16)


def kernel(x_prompt, x_sample, mem_prompt, state_pool, state_conv, state_ssm, cache_mem_k, cache_mem_v,
           norm_g, w_in, pool_w, pool_scale, conv_w, conv_b, dt_bias, a_log, d_skip, ssd_norm_g,
           mem_norm_g, w_mem_k, w_mem_v, w_out, final_norm_g):
    batch, seq_p, _ = x_prompt.shape
    dec_batch, seq_s, _ = x_sample.shape

    w_in_r = _reorder_w_in(w_in)
    w_kv = jnp.concatenate([w_mem_k, w_mem_v], axis=-1).astype(BF16)
    sel = _expand_selector()
    fng = final_norm_g.reshape(1, D_MODEL)
    ng = norm_g.reshape(DEPTH, 1, D_MODEL)
    lw = (w_out.astype(BF16), pool_w.astype(BF16), pool_scale.reshape(DEPTH, 1, POOL_WIDTH), conv_w,
          conv_b.reshape(DEPTH, 1, CONV_DIM),
          jnp.tile(dt_bias, (1, DT_COPIES)).reshape(DEPTH, 1, LANES),
          jnp.tile(-jnp.exp(a_log), (1, DT_COPIES)).reshape(DEPTH, 1, LANES),
          jnp.repeat(d_skip, SSD_HEAD_DIM, axis=-1).reshape(DEPTH, 1, SSD_WIDTH),
          ssd_norm_g.reshape(DEPTH, 1, SSD_WIDTH))

    k_flat, v_flat, k_dense, v_dense = _mem_kv(
        mem_prompt.reshape(batch * N_MEM, D_MODEL), mem_norm_g, w_kv)
    y_p, pool_p, conv_p, ssm_p = _prompt_stack(
        x_prompt.reshape(batch * seq_p, D_MODEL), k_dense, v_dense, ng, w_in_r, lw, fng, sel,
        batch, seq_p)

    hs = x_sample.reshape(dec_batch * seq_s, D_MODEL)
    st_ssm = state_ssm.reshape(DEPTH, dec_batch, SSD_WIDTH, D_STATE)
    ck = cache_mem_k.reshape(DEPTH, dec_batch, N_MEM * XA_HEADS, XA_HEAD_DIM)
    cv = cache_mem_v.reshape(DEPTH, dec_batch, N_MEM * XA_HEADS, XA_HEAD_DIM)
    stacked = None
    for layer in range(DEPTH):
        proj_s = _norm_proj(hs, ng, w_in_r, layer, 512)
        hs, *stacked = _sample_mixer(proj_s, hs, ck, cv, state_pool, state_conv, st_ssm, lw, fng, sel,
                                     stacked, layer, 8, seq_s)
    pool_s, conv_s, ssm_s = stacked

    mem_shape = (DEPTH, batch, N_MEM, XA_HEADS, XA_HEAD_DIM)
    return (y_p.reshape(batch, seq_p, D_MODEL), hs.reshape(dec_batch, seq_s, D_MODEL),
            pool_p, conv_p, ssm_p.reshape(DEPTH, batch, SSD_HEADS, SSD_HEAD_DIM, D_STATE),
            k_flat.reshape(mem_shape), v_flat.reshape(mem_shape),
            pool_s, conv_s, ssm_s.reshape(DEPTH, dec_batch, SSD_HEADS, SSD_HEAD_DIM, D_STATE))
```

```python
import functools
import math

import jax
import jax.numpy as jnp
from jax import lax
from jax.experimental import pallas as pl
from jax.experimental.pallas import tpu as pltpu

F32 = jnp.float32
BF16 = jnp.bfloat16

D_MODEL = 1024
DEPTH = 4
PAST_LEN = 16384
POOL_WIDTH = 512
POOL_WINDOWS = (2, 4, 8, 16)
POOL_GROUP_DIM = 128
POOL_STATE_LEN = 15
SSD_WIDTH = 1024
SSD_HEAD_DIM = 64
SSD_HEADS = 16
SSD_GROUPS = 2
SSD_HEADS_PER_GROUP = 8
SSD_GROUP_WIDTH = SSD_WIDTH // SSD_GROUPS
D_STATE = 128
CONV_WIDTH = 4
CONV_DIM = 1536
XA_WIDTH = 512
XA_HEADS = 4
XA_HEAD_DIM = 128
N_MEM = 256
D_MIX = 2048
EPS = 1e-6
LOG2E = 1.4426950408889634

LANES = 128
SUBLANES = 8
POOL_HIST = 32
CONV_HIST = 8
CHUNK = 128
PROMPT_STEP = 256

COL_U = 0
COL_GP = COL_U + POOL_WIDTH
COL_Z = COL_GP + POOL_WIDTH
COL_XBC = COL_Z + SSD_WIDTH
COL_Q = COL_XBC + CONV_DIM
COL_GX = COL_Q + XA_WIDTH
COL_DT = COL_GX + XA_WIDTH
DT_COPIES = LANES // SSD_HEADS
PROJ_WIDTH = COL_DT + 2 * LANES

EXPAND_GROUPS = (0, 0, 1, 1, 2, 2, 0, 1)
EXPAND_TERM = (0, 1, 0, 1, 0, 1, 2, 2)
N_EXPAND = 3

VMEM_LIMIT_BYTES = 56 * 1024 * 1024


def _silu(x):
    h = 0.5 * x
    return h + h * jnp.tanh(h)


def _softplus(x):
    return jnp.maximum(x, 0.0) + jnp.log1p(jnp.exp(-jnp.abs(x)))


def _bf16_terms(v, n):
    terms = []
    for _ in range(n):
        t = v.astype(BF16).astype(F32)
        terms.append(t)
        v = v - t
    return terms


def _dot(a, b):
    return jnp.dot(a, b, preferred_element_type=F32)


def _dot_nt(a, b):
    return lax.dot_general(a, b, (((1,), (1,)), ((), ())), preferred_element_type=F32)


def _dot_tn(a, b):
    return lax.dot_general(a, b, (((0,), (0,)), ((), ())), preferred_element_type=F32)


def _exact_left_dot(sel_bf16, v):
    return sum(_dot(sel_bf16, t.astype(BF16)) for t in _bf16_terms(v, 3))


def _rms_scale(x, g):
    return x * lax.rsqrt(jnp.mean(x * x, axis=-1, keepdims=True) + EPS) * g


def _init_history(extp_ref, extc_ref, ssm_ref, init_refs):
    nb = extp_ref.shape[0]
    extp_ref[:, 0:POOL_HIST, :] = jnp.zeros((nb, POOL_HIST, POOL_WIDTH), F32)
    if init_refs is None:
        extc_ref[:, 0:CONV_HIST, :] = jnp.zeros((nb, CONV_HIST, CONV_DIM), F32)
        ssm_ref[...] = jnp.zeros(ssm_ref.shape, F32)
    else:
        pool0_ref, conv0_ref, ssm0_ref = init_refs
        for r in range(POOL_STATE_LEN):
            extp_ref[:, POOL_HIST - POOL_STATE_LEN + r, :] = pool0_ref[r]
        for r in range(CONV_WIDTH - 1):
            extc_ref[:, CONV_HIST - (CONV_WIDTH - 1) + r, :] = conv0_ref[r]
        ssm_ref[...] = ssm0_ref[...]


def _pool_heads(proj_ref, extp_ref, pa_ref, pb_ref, mix_ref, poolw_ref, pscale_ref, nb, seq, pos0):
    t = nb * seq
    n = POOL_HIST + seq
    extp_ref[:, POOL_HIST:n, :] = proj_ref[:, COL_U:COL_U + POOL_WIDTH].reshape(nb, seq, POOL_WIDTH)
    g = POOL_GROUP_DIM
    pa_ref[:, 8:n, :] = extp_ref[:, 8:n, :] + extp_ref[:, 7:n - 1, :]
    pb_ref[:, 16:n, g:4 * g] = pa_ref[:, 16:n, g:4 * g] + pa_ref[:, 14:n - 2, g:4 * g]
    pa_ref[:, 24:n, 2 * g:4 * g] = pb_ref[:, 24:n, 2 * g:4 * g] + pb_ref[:, 20:n - 4, 2 * g:4 * g]
    pb_ref[:, 32:n, 3 * g:4 * g] = pa_ref[:, 32:n, 3 * g:4 * g] + pa_ref[:, 24:n - 8, 3 * g:4 * g]
    tok = lax.broadcasted_iota(jnp.int32, (nb, seq, LANES), 1)
    pos1 = pos0 + tok + 1
    for gi, w in enumerate(POOL_WINDOWS):
        sl = slice(gi * g, (gi + 1) * g)
        src = pa_ref if gi % 2 == 0 else pb_ref
        cnt = jnp.minimum(pos1, w).astype(F32)
        d = (src[:, POOL_HIST:n, sl] / cnt - extp_ref[:, POOL_HIST:n, sl]).reshape(t, g)
        o = _dot(d.astype(BF16), poolw_ref[gi]) * pscale_ref[:, sl]
        gp = proj_ref[:, COL_GP + gi * g:COL_GP + (gi + 1) * g]
        mix_ref[:, sl] = (o * _silu(gp)).astype(BF16)


def _conv_silu(proj_ref, extc_ref, xc_ref, convw_ref, convb_ref, nb, seq):
    t = nb * seq
    extc_ref[:, CONV_HIST:CONV_HIST + seq, :] = proj_ref[:, COL_XBC:COL_XBC + CONV_DIM].reshape(
        nb, seq, CONV_DIM)
    cblk = 256
    for j in range(CONV_DIM // cblk):
        sl = slice(j * cblk, (j + 1) * cblk)
        if nb == 1:
            ext = extc_ref[0, :, sl]
            acc = convw_ref[0:1, sl] * ext
            for k in range(1, CONV_WIDTH):
                acc = pltpu.roll(acc, 1, 0) + convw_ref[k:k + 1, sl] * ext
            acc = acc[CONV_HIST:CONV_HIST + seq] + convb_ref[:, sl]
        else:
            acc = convb_ref[:, sl][None]
            for k in range(CONV_WIDTH):
                r0 = CONV_HIST - (CONV_WIDTH - 1) + k
                acc = acc + convw_ref[k:k + 1, sl][None] * extc_ref[:, r0:r0 + seq, sl]
        xc_ref[:, sl] = _silu(acc).reshape(t, cblk)


def _expand_heads(vals, sel_ref):
    t = vals[0].shape[0]
    terms = [_bf16_terms(v, 1 + max(tm for gq, tm in zip(EXPAND_GROUPS, EXPAND_TERM) if gq == i))
             for i, v in enumerate(vals)]
    group = lax.shift_right_logical(lax.broadcasted_iota(jnp.int32, (t, LANES), 1), 4)
    packed = terms[EXPAND_GROUPS[0]][EXPAND_TERM[0]]
    for gidx in range(1, len(EXPAND_GROUPS)):
        packed = jnp.where(group == gidx, terms[EXPAND_GROUPS[gidx]][EXPAND_TERM[gidx]], packed)
    out = _dot(packed.astype(BF16), sel_ref[...])
    return [out[:, i * SSD_WIDTH:(i + 1) * SSD_WIDTH] for i in range(len(vals))]


def _ssd_chunk(proj_ref, xc_ref, y_ref, ssm_ref, dtb_ref, arow_ref, sel_ref, nb, seq):
    t = nb * seq
    dt = _softplus(proj_ref[:, COL_DT:COL_DT + LANES] + dtb_ref[...])
    a = dt * arow_ref[...]
    ri = lax.broadcasted_iota(jnp.int32, (t, t), 0)
    ci = lax.broadcasted_iota(jnp.int32, (t, t), 1)
    if nb > 1:
        seq_start = ri - lax.rem(ri, seq)
        causal = jnp.bitwise_or(ci - seq_start, ri - ci) >= 0
    else:
        causal = ci <= ri
    tri = jnp.where(causal, 1.0, 0.0).astype(BF16)
    cum = _exact_left_dot(tri, a)
    cum2 = cum * LOG2E
    pad = LANES - t
    if pad:
        cum2_t = jnp.concatenate([cum2, jnp.zeros((pad, LANES), F32)], axis=0).T[:, 0:t]
    else:
        cum2_t = cum2.T
    cum3 = cum.reshape(nb, seq, LANES)
    cum_last = cum3[:, seq - 1:seq, :]
    e_tok = jnp.exp(cum)
    w_tok = (jnp.exp(cum_last - cum3) * dt.reshape(nb, seq, LANES)).reshape(t, LANES)
    dec = jnp.exp(cum_last)
    e_exp, w_exp, dt_exp = _expand_heads([e_tok, w_tok, dt], sel_ref)

    lane = lax.broadcasted_iota(jnp.int32, (t, LANES), 1)
    low_half = lane < SSD_HEAD_DIM

    for g in range(SSD_GROUPS):
        b0 = SSD_WIDTH + g * D_STATE
        c0 = SSD_WIDTH + SSD_GROUPS * D_STATE + g * D_STATE
        bg = xc_ref[:, b0:b0 + D_STATE].astype(BF16)
        cg = xc_ref[:, c0:c0 + D_STATE].astype(BF16)
        cb = jnp.where(causal, _dot_nt(cg, bg), 0.0)
        for pp in range(SSD_HEADS_PER_GROUP // 2):
            h0 = g * SSD_HEADS_PER_GROUP + 2 * pp
            ms = []
            for hh in (h0, h0 + 1):
                seg2 = cum2[:, hh:hh + 1] - cum2_t[hh:hh + 1, :]
                ms.append((cb * jnp.exp2(jnp.minimum(seg2, 0.0))).astype(BF16))
            col = (h0 // 2) * LANES
            xdt = xc_ref[:, col:col + LANES] * dt_exp[:, col:col + LANES]
            rhs = jnp.concatenate(
                [jnp.where(low_half, xdt, 0.0), jnp.where(low_half, 0.0, xdt)], axis=0).astype(BF16)
            y_ref[:, col:col + LANES] = _dot(jnp.concatenate(ms, axis=1), rhs)
        gsl = slice(g * SSD_GROUP_WIDTH, (g + 1) * SSD_GROUP_WIDTH)
        for b in range(nb):
            rows = slice(b * seq, (b + 1) * seq)
            s_old = ssm_ref[b, gsl, :]
            yi = _dot_nt(cg[rows], s_old.astype(BF16))
            y_ref[rows, gsl] = y_ref[rows, gsl] + yi * e_exp[rows, gsl]
            xw = (xc_ref[rows, gsl] * w_exp[rows, gsl]).astype(BF16)
            upd = _dot_tn(xw, bg[rows])
            for r in range(SSD_HEADS_PER_GROUP):
                hh = g * SSD_HEADS_PER_GROUP + r
                hs = slice(r * SSD_HEAD_DIM, (r + 1) * SSD_HEAD_DIM)
                r0 = g * SSD_GROUP_WIDTH + r * SSD_HEAD_DIM
                ssm_ref[b, r0:r0 + SSD_HEAD_DIM, :] = s_old[hs] * dec[b, :, hh:hh + 1] + upd[hs]


def _gated_group_norm(proj_ref, xc_ref, y_ref, mix_ref, dskip_ref, normg_ref):
    z = proj_ref[:, COL_Z:COL_Z + SSD_WIDTH]
    y = (y_ref[...] + dskip_ref[...] * xc_ref[:, 0:SSD_WIDTH]) * _silu(z)
    for g in range(SSD_GROUPS):
        gsl = slice(g * SSD_GROUP_WIDTH, (g + 1) * SSD_GROUP_WIDTH)
        mix_ref[:, POOL_WIDTH + g * SSD_GROUP_WIDTH:POOL_WIDTH + (g + 1) * SSD_GROUP_WIDTH] = (
            _rms_scale(y[:, gsl], normg_ref[:, gsl])).astype(BF16)


def _softmax_rows(s):
    s = s - jnp.max(s, axis=-1, keepdims=True)
    p = jnp.exp(s)
    return p / jnp.sum(p, axis=-1, keepdims=True)


def _xattn_dense(proj_ref, mix_ref, k_ref, v_ref):
    inv_sqrt_d = 1.0 / math.sqrt(XA_HEAD_DIM)
    for hh in range(XA_HEADS):
        sl = slice(hh * XA_HEAD_DIM, (hh + 1) * XA_HEAD_DIM)
        q = proj_ref[:, COL_Q + hh * XA_HEAD_DIM:COL_Q + (hh + 1) * XA_HEAD_DIM].astype(BF16)
        p = _softmax_rows(_dot_nt(q, k_ref[:, sl]) * inv_sqrt_d)
        o = _dot(p.astype(BF16), v_ref[:, sl])
        gx = proj_ref[:, COL_GX + hh * XA_HEAD_DIM:COL_GX + (hh + 1) * XA_HEAD_DIM]
        m0 = POOL_WIDTH + SSD_WIDTH + hh * XA_HEAD_DIM
        mix_ref[:, m0:m0 + XA_HEAD_DIM] = (o * _silu(gx)).astype(BF16)


def _xattn_rowmajor(proj_ref, mix_ref, k_ref, v_ref, nb, seq):
    t = nb * seq
    inv_sqrt_d = 1.0 / math.sqrt(XA_HEAD_DIM)
    q4 = jnp.concatenate(
        [proj_ref[:, COL_Q + hh * XA_HEAD_DIM:COL_Q + (hh + 1) * XA_HEAD_DIM].astype(BF16).reshape(
            nb, seq, XA_HEAD_DIM) for hh in range(XA_HEADS)], axis=1)
    kf = k_ref[...].astype(BF16)
    vf = v_ref[...].astype(BF16)
    s = lax.dot_general(q4, kf, (((2,), (2,)), ((0,), (0,))), preferred_element_type=F32) * inv_sqrt_d
    shape = (nb, XA_HEADS * seq, N_MEM * XA_HEADS)
    row = lax.broadcasted_iota(jnp.int32, shape, 1)
    col = lax.broadcasted_iota(jnp.int32, shape, 2)
    own_head = jnp.bitwise_and(col, XA_HEADS - 1) * seq
    valid = jnp.bitwise_or(row - own_head, own_head + (seq - 1) - row) >= 0
    p = _softmax_rows(jnp.where(valid, s, -1e30))
    o4 = lax.dot_general(p.astype(BF16), vf, (((2,), (1,)), ((0,), (0,))), preferred_element_type=F32)
    for hh in range(XA_HEADS):
        o = o4[:, hh * seq:(hh + 1) * seq, :].reshape(t, XA_HEAD_DIM)
        gx = proj_ref[:, COL_GX + hh * XA_HEAD_DIM:COL_GX + (hh + 1) * XA_HEAD_DIM]
        m0 = POOL_WIDTH + SSD_WIDTH + hh * XA_HEAD_DIM
        mix_ref[:, m0:m0 + XA_HEAD_DIM] = (o * _silu(gx)).astype(BF16)


def _mem_kv_kernel(mem_ref, g_ref, w_ref, kf_ref, vf_ref, kd_ref, vd_ref):
    xn = _rms_scale(mem_ref[...], g_ref[...]).astype(BF16)
    kv = _dot(xn, w_ref[...])
    kd_ref[...] = kv[:, 0:XA_WIDTH].astype(BF16)
    vd_ref[...] = kv[:, XA_WIDTH:2 * XA_WIDTH].astype(BF16)
    for hh in range(XA_HEADS):
        kf_ref[pl.ds(hh, N_MEM, stride=XA_HEADS), :] = kv[:, hh * XA_HEAD_DIM:(hh + 1) * XA_HEAD_DIM]
        vf_ref[pl.ds(hh, N_MEM, stride=XA_HEADS), :] = kv[
            :, XA_WIDTH + hh * XA_HEAD_DIM:XA_WIDTH + (hh + 1) * XA_HEAD_DIM]


def _mem_kv(mem2d, mem_norm_g, w_kv):
    batch = mem2d.shape[0] // N_MEM
    flat = jax.ShapeDtypeStruct((DEPTH, batch, N_MEM * XA_HEADS, XA_HEAD_DIM), F32)
    dense = jax.ShapeDtypeStruct((DEPTH, batch, N_MEM, XA_WIDTH), BF16)
    flat_spec = pl.BlockSpec((None, None, N_MEM * XA_HEADS, XA_HEAD_DIM), lambda l, b: (l, b, 0, 0))
    dense_spec = pl.BlockSpec((None, None, N_MEM, XA_WIDTH), lambda l, b: (l, b, 0, 0))
    return pl.pallas_call(
        _mem_kv_kernel,
        grid=(DEPTH, batch),
        in_specs=[
            pl.BlockSpec((N_MEM, D_MODEL), lambda l, b: (b, 0)),
            pl.BlockSpec((None, 1, D_MODEL), lambda l, b: (l, 0, 0)),
            pl.BlockSpec((None, D_MODEL, 2 * XA_WIDTH), lambda l, b: (l, 0, 0)),
        ],
        out_specs=[flat_spec, flat_spec, dense_spec, dense_spec],
        out_shape=[flat, flat, dense, dense],
        compiler_params=pltpu.CompilerParams(dimension_semantics=("arbitrary", "arbitrary")),
        name="mem_kv",
    )(mem2d, mem_norm_g.reshape(DEPTH, 1, D_MODEL), w_kv)


def _prompt_kernel(x_ref, k_ref, v_ref, ng_ref, win_ref, wout_ref, poolw_ref, pscale_ref, convw_ref,
                   convb_ref, dtb_ref, arow_ref, dskip_ref, normg_ref, fng_ref, sel_ref,
                   y_out_ref, poolo_ref, convo_ref, ssm_ref,
                   h_ref, proj_ref, extp_ref, pa_ref, pb_ref, extc_ref, xc_ref, y_ref, mix_ref):
    layer = pl.program_id(1)
    c = pl.program_id(2)
    rows = pl.ds(pl.multiple_of(c * PROMPT_STEP, PROMPT_STEP), PROMPT_STEP)

    @pl.when(layer == 0)
    def _load():
        h_ref[rows, :] = x_ref[...]

    @pl.when(c == 0)
    def _init():
        _init_history(extp_ref, extc_ref, ssm_ref, None)

    h = h_ref[rows, :]
    proj_ref[...] = _dot(_rms_scale(h, ng_ref[...]).astype(BF16), win_ref[...])

    _pool_heads(proj_ref, extp_ref, pa_ref, pb_ref, mix_ref, poolw_ref, pscale_ref, 1, PROMPT_STEP,
                c * PROMPT_STEP)
    poolo_ref[...] = extp_ref[:, PROMPT_STEP + POOL_HIST - POOL_STATE_LEN:PROMPT_STEP + POOL_HIST, :]
    extp_ref[:, 0:POOL_HIST, :] = extp_ref[:, PROMPT_STEP:PROMPT_STEP + POOL_HIST, :]

    _conv_silu(proj_ref, extc_ref, xc_ref, convw_ref, convb_ref, 1, PROMPT_STEP)
    convo_ref[...] = extc_ref[:, PROMPT_STEP + CONV_HIST - (CONV_WIDTH - 1):PROMPT_STEP + CONV_HIST, :]
    extc_ref[:, CONV_HIST - (CONV_WIDTH - 1):CONV_HIST, :] = extc_ref[
        :, PROMPT_STEP + CONV_HIST - (CONV_WIDTH - 1):PROMPT_STEP + CONV_HIST, :]

    for s in range(PROMPT_STEP // CHUNK):
        sub = pl.ds(s * CHUNK, CHUNK)
        _ssd_chunk(proj_ref.at[sub], xc_ref.at[sub], y_ref.at[sub], ssm_ref, dtb_ref, arow_ref,
                   sel_ref, 1, CHUNK)
    _gated_group_norm(proj_ref, xc_ref, y_ref, mix_ref, dskip_ref, normg_ref)
    _xattn_dense(proj_ref, mix_ref, k_ref, v_ref)

    h_new = h + _dot(mix_ref[...], wout_ref[...])
    h_ref[rows, :] = h_new

    @pl.when(layer == DEPTH - 1)
    def _final():
        y_out_ref[...] = _rms_scale(h_new, fng_ref[...])


def _prompt_stack(x2d, k_dense, v_dense, ng, w_in_r, lw, fng, sel, batch, seq_len):
    n_steps = seq_len // PROMPT_STEP
    assert n_steps * PROMPT_STEP == seq_len
    last = DEPTH - 1

    def lay3(shape):
        return pl.BlockSpec((None,) + shape, lambda b, l, c: (l, 0, 0))

    def per_seq(shape):
        return pl.BlockSpec((None, None) + shape, lambda b, l, c: (l, b, 0, 0))

    def state_out(shape):
        return pl.BlockSpec((None, 1) + shape, lambda b, l, c: (l, b, 0, 0))

    x_spec = pl.BlockSpec((PROMPT_STEP, D_MODEL),
                          lambda b, l, c: (b * n_steps + jnp.where(l == 0, c, n_steps - 1), 0))
    y_spec = pl.BlockSpec((PROMPT_STEP, D_MODEL),
                          lambda b, l, c: (b * n_steps + jnp.where(l == last, c, 0), 0))
    in_specs = [
        x_spec,
        per_seq((N_MEM, XA_WIDTH)), per_seq((N_MEM, XA_WIDTH)),
        lay3((1, D_MODEL)),
        lay3((D_MODEL, PROJ_WIDTH)),
        lay3((D_MIX, D_MODEL)),
        pl.BlockSpec((None, 4, POOL_GROUP_DIM, POOL_GROUP_DIM), lambda b, l, c: (l, 0, 0, 0)),
        lay3((1, POOL_WIDTH)),
        lay3((CONV_WIDTH, CONV_DIM)),
        lay3((1, CONV_DIM)),
        lay3((1, LANES)),
        lay3((1, LANES)),
        lay3((1, SSD_WIDTH)),
        lay3((1, SSD_WIDTH)),
        pl.BlockSpec((1, D_MODEL), lambda b, l, c: (0, 0)),
        pl.BlockSpec((LANES, N_EXPAND * SSD_WIDTH), lambda b, l, c: (0, 0)),
    ]
    out_specs = [
        y_spec,
        state_out((POOL_STATE_LEN, POOL_WIDTH)),
        state_out((CONV_WIDTH - 1, CONV_DIM)),
        state_out((SSD_WIDTH, D_STATE)),
    ]
    out_shape = [
        jax.ShapeDtypeStruct((batch * seq_len, D_MODEL), F32),
        jax.ShapeDtypeStruct((DEPTH, batch, POOL_STATE_LEN, POOL_WIDTH), F32),
        jax.ShapeDtypeStruct((DEPTH, batch, CONV_WIDTH - 1, CONV_DIM), F32),
        jax.ShapeDtypeStruct((DEPTH, batch, SSD_WIDTH, D_STATE), F32),
    ]
    scratch = [
        pltpu.VMEM((seq_len, D_MODEL), F32),
        pltpu.VMEM((PROMPT_STEP, PROJ_WIDTH), F32),
        pltpu.VMEM((1, POOL_HIST + PROMPT_STEP, POOL_WIDTH), F32),
        pltpu.VMEM((1, POOL_HIST + PROMPT_STEP, POOL_WIDTH), F32),
        pltpu.VMEM((1, POOL_HIST + PROMPT_STEP, POOL_WIDTH), F32),
        pltpu.VMEM((1, CONV_HIST + PROMPT_STEP, CONV_DIM), F32),
        pltpu.VMEM((PROMPT_STEP, CONV_DIM), F32),
        pltpu.VMEM((PROMPT_STEP, SSD_WIDTH), F32),
        pltpu.VMEM((PROMPT_STEP, D_MIX), BF16),
    ]
    return pl.pallas_call(
        _prompt_kernel,
        grid=(batch, DEPTH, n_steps),
        in_specs=in_specs,
        out_specs=out_specs,
        out_shape=out_shape,
        scratch_shapes=scratch,
        compiler_params=pltpu.CompilerParams(
            dimension_semantics=("arbitrary", "arbitrary", "arbitrary"),
            vmem_limit_bytes=VMEM_LIMIT_BYTES),
        name="prompt_stack",
    )(x2d, k_dense, v_dense, ng, w_in_r, *lw, fng, sel)


def _norm_proj_kernel(x_ref, g_ref, w_ref, o_ref):
    o_ref[...] = _dot(_rms_scale(x_ref[...], g_ref[...]).astype(BF16), w_ref[...])


def _norm_proj(x, g_all, w_all, layer, tm):
    t, d = x.shape
    n = w_all.shape[-1]
    assert t % tm == 0
    return pl.pallas_call(
        _norm_proj_kernel,
        grid=(t // tm,),
        in_specs=[
            pl.BlockSpec((tm, d), lambda i: (i, 0)),
            pl.BlockSpec((None, 1, d), lambda i: (layer, 0, 0)),
            pl.BlockSpec((None, d, n), lambda i: (layer, 0, 0)),
        ],
        out_specs=pl.BlockSpec((tm, n), lambda i: (i, 0)),
        out_shape=jax.ShapeDtypeStruct((t, n), F32),
        compiler_params=pltpu.CompilerParams(
            dimension_semantics=("arbitrary",), vmem_limit_bytes=VMEM_LIMIT_BYTES),
        name="norm_proj",
    )(x, g_all, w_all)


def _sample_kernel(*refs, nb, seq, n_alias, final):
    (proj_ref, h_ref, k_ref, v_ref, pool0_ref, conv0_ref, ssm0_ref,
     wout_ref, poolw_ref, pscale_ref, convw_ref, convb_ref, dtb_ref, arow_ref, dskip_ref,
     normg_ref, fng_ref, sel_ref) = refs[:18]
    (hout_ref, poolo_ref, convo_ref, ssm_ref,
     extp_ref, pa_ref, pb_ref, extc_ref, xc_ref, y_ref, mix_ref) = refs[18 + n_alias:]

    _init_history(extp_ref, extc_ref, ssm_ref, (pool0_ref, conv0_ref, ssm0_ref))
    _pool_heads(proj_ref, extp_ref, pa_ref, pb_ref, mix_ref, poolw_ref, pscale_ref, nb, seq, PAST_LEN)
    poolo_ref[...] = extp_ref[:, seq + POOL_HIST - POOL_STATE_LEN:seq + POOL_HIST, :]
    _conv_silu(proj_ref, extc_ref, xc_ref, convw_ref, convb_ref, nb, seq)
    convo_ref[...] = extc_ref[:, seq + CONV_HIST - (CONV_WIDTH - 1):seq + CONV_HIST, :]
    _ssd_chunk(proj_ref, xc_ref, y_ref, ssm_ref, dtb_ref, arow_ref, sel_ref, nb, seq)
    _gated_group_norm(proj_ref, xc_ref, y_ref, mix_ref, dskip_ref, normg_ref)
    _xattn_rowmajor(proj_ref, mix_ref, k_ref, v_ref, nb, seq)
    h_new = h_ref[...] + _dot(mix_ref[...], wout_ref[...])
    hout_ref[...] = _rms_scale(h_new, fng_ref[...]) if final else h_new


def _sample_mixer(proj, h, k_all, v_all, st_pool, st_conv, st_ssm, lw, fng, sel, stacked, layer, nb, seq):
    n_seq = h.shape[0] // seq
    t = nb * seq
    assert n_seq % nb == 0
    final = layer == DEPTH - 1

    def per_seq(shape):
        return pl.BlockSpec((None, nb) + shape, lambda i: (layer, i) + (0,) * len(shape))

    def lay(shape):
        return pl.BlockSpec((None,) + shape, lambda i: (layer,) + (0,) * len(shape))

    def rows_by_seq(n_rows, width):
        return pl.BlockSpec((None, n_rows, nb, width), lambda i: (layer, 0, i, 0))

    rows = lambda width: pl.BlockSpec((t, width), lambda i: (i, 0))
    in_specs = [
        rows(PROJ_WIDTH), rows(D_MODEL),
        per_seq((N_MEM * XA_HEADS, XA_HEAD_DIM)), per_seq((N_MEM * XA_HEADS, XA_HEAD_DIM)),
        rows_by_seq(POOL_STATE_LEN, POOL_WIDTH), rows_by_seq(CONV_WIDTH - 1, CONV_DIM),
        per_seq((SSD_WIDTH, D_STATE)),
        lay((D_MIX, D_MODEL)), lay((4, POOL_GROUP_DIM, POOL_GROUP_DIM)), lay((1, POOL_WIDTH)),
        lay((CONV_WIDTH, CONV_DIM)), lay((1, CONV_DIM)), lay((1, LANES)), lay((1, LANES)),
        lay((1, SSD_WIDTH)), lay((1, SSD_WIDTH)),
        pl.BlockSpec((1, D_MODEL), lambda i: (0, 0)),
        pl.BlockSpec((LANES, N_EXPAND * SSD_WIDTH), lambda i: (0, 0)),
    ]
    args = [proj, h, k_all, v_all, st_pool, st_conv, st_ssm, *lw, fng, sel]
    aliases = {}
    n_alias = 0
    if stacked is not None:
        n_alias = len(stacked)
        for j, arr in enumerate(stacked):
            aliases[len(args)] = 1 + j
            in_specs.append(pl.BlockSpec(memory_space=pl.ANY))
            args.append(arr)
    out_specs = [
        rows(D_MODEL),
        per_seq((POOL_STATE_LEN, POOL_WIDTH)), per_seq((CONV_WIDTH - 1, CONV_DIM)),
        per_seq((SSD_WIDTH, D_STATE)),
    ]
    out_shape = [
        jax.ShapeDtypeStruct(h.shape, F32),
        jax.ShapeDtypeStruct((DEPTH, n_seq, POOL_STATE_LEN, POOL_WIDTH), F32),
        jax.ShapeDtypeStruct((DEPTH, n_seq, CONV_WIDTH - 1, CONV_DIM), F32),
        jax.ShapeDtypeStruct((DEPTH, n_seq, SSD_WIDTH, D_STATE), F32),
    ]
    scratch = [
        pltpu.VMEM((nb, POOL_HIST + seq, POOL_WIDTH), F32),
        pltpu.VMEM((nb, POOL_HIST + seq, POOL_WIDTH), F32),
        pltpu.VMEM((nb, POOL_HIST + seq, POOL_WIDTH), F32),
        pltpu.VMEM((nb, CONV_HIST + seq, CONV_DIM), F32),
        pltpu.VMEM((t, CONV_DIM), F32),
        pltpu.VMEM((t, SSD_WIDTH), F32),
        pltpu.VMEM((t, D_MIX), BF16),
    ]
    return pl.pallas_call(
        functools.partial(_sample_kernel, nb=nb, seq=seq, n_alias=n_alias, final=final),
        grid=(n_seq // nb,),
        in_specs=in_specs,
        out_specs=out_specs,
        out_shape=out_shape,
        scratch_shapes=scratch,
        input_output_aliases=aliases,
        compiler_params=pltpu.CompilerParams(
            dimension_semantics=("arbitrary",), vmem_limit_bytes=VMEM_LIMIT_BYTES),
        name="sample_mixer",
    )(*args)


REORDER_BLOCK = 256
N_ALIGNED_BLOCKS = COL_Q // REORDER_BLOCK
N_MAIN_BLOCKS = COL_DT // REORDER_BLOCK


def _reorder_w_in_kernel(wt_ref, o_ref):
    j = pl.program_id(1)
    t = wt_ref[0].T.astype(BF16)

    @pl.when(j < N_MAIN_BLOCKS)
    def _main():
        o_ref[...] = t

    @pl.when(j == N_MAIN_BLOCKS)
    def _dt():
        o_ref[:, 0:LANES] = jnp.concatenate([t[:, 0:SSD_HEADS]] * DT_COPIES, axis=-1)
        o_ref[:, LANES:REORDER_BLOCK] = jnp.zeros((D_MODEL, REORDER_BLOCK - LANES), BF16)


def _reorder_w_in(w_in):
    depth, d, n = w_in.shape
    src_dt = COL_Q
    wt = jnp.swapaxes(w_in, 1, 2)

    def src_row(l, j):
        shifted = REORDER_BLOCK * j + jnp.where(j >= N_ALIGNED_BLOCKS, SSD_HEADS, 0)
        return (l, pl.multiple_of(jnp.where(j == N_MAIN_BLOCKS, src_dt, shifted), SSD_HEADS), 0)

    return pl.pallas_call(
        _reorder_w_in_kernel,
        grid=(depth, N_MAIN_BLOCKS + 1),
        in_specs=[pl.BlockSpec((pl.Element(1), pl.Element(REORDER_BLOCK), pl.Element(d)), src_row)],
        out_specs=pl.BlockSpec((None, d, REORDER_BLOCK), lambda l, j: (l, 0, j)),
        out_shape=jax.ShapeDtypeStruct((depth, d, PROJ_WIDTH), BF16),
        compiler_params=pltpu.CompilerParams(dimension_semantics=("arbitrary", "arbitrary")),
        name="reorder_w_in",
    )(wt)


def _expand_selector():
    lane = jnp.arange(LANES, dtype=jnp.int32)
    quantity = jnp.asarray(EXPAND_GROUPS, jnp.int32)[lane // SSD_HEADS]
    head = lane % SSD_HEADS
    col = jnp.arange(N_EXPAND * SSD_WIDTH, dtype=jnp.int32)
    col_quantity = col // SSD_WIDTH
    col_head = (col % SSD_WIDTH) // SSD_HEAD_DIM
    return jnp.logical_and(quantity[:, None] == col_quantity[None, :],
                           head[:, None] == col_head[None, :]).astype(BF16)


def kernel(x_prompt, x_sample, mem_prompt, state_pool, state_conv, state_ssm, cache_mem_k, cache_mem_v,
           norm_g, w_in, pool_w, pool_scale, conv_w, conv_b, dt_bias, a_log, d_skip, ssd_norm_g,
           mem_norm_g, w_mem_k, w_mem_v, w_out, final_norm_g):
    batch, seq_p, _ = x_prompt.shape
    dec_batch, seq_s, _ = x_sample.shape

    w_in_r = _reorder_w_in(w_in)
    w_kv = jnp.concatenate([w_mem_k, w_mem_v], axis=-1).astype(BF16)
    sel = _expand_selector()
    fng = final_norm_g.reshape(1, D_MODEL)
    ng = norm_g.reshape(DEPTH, 1, D_MODEL)
    lw = (w_out.astype(BF16), pool_w.astype(BF16), pool_scale.reshape(DEPTH, 1, POOL_WIDTH), conv_w,
          conv_b.reshape(DEPTH, 1, CONV_DIM),
          jnp.tile(dt_bias, (1, DT_COPIES)).reshape(DEPTH, 1, LANES),
          jnp.tile(-jnp.exp(a_log), (1, DT_COPIES)).reshape(DEPTH, 1, LANES),
          jnp.repeat(d_skip, SSD_HEAD_DIM, axis=-1).reshape(DEPTH, 1, SSD_WIDTH),
          ssd_norm_g.reshape(DEPTH, 1, SSD_WIDTH))

    k_flat, v_flat, k_dense, v_dense = _mem_kv(
        mem_prompt.reshape(batch * N_MEM, D_MODEL), mem_norm_g, w_kv)
    y_p, pool_p, conv_p, ssm_p = _prompt_stack(
        x_prompt.reshape(batch * seq_p, D_MODEL), k_dense, v_dense, ng, w_in_r, lw, fng, sel,
        batch, seq_p)

    hs = x_sample.reshape(dec_batch * seq_s, D_MODEL)
    st_ssm = state_ssm.reshape(DEPTH, dec_batch, SSD_WIDTH, D_STATE)
    st_pool = jnp.swapaxes(state_pool, 1, 2)
    st_conv = jnp.swapaxes(state_conv, 1, 2)
    ck = cache_mem_k.reshape(DEPTH, dec_batch, N_MEM * XA_HEADS, XA_HEAD_DIM)
    cv = cache_mem_v.reshape(DEPTH, dec_batch, N_MEM * XA_HEADS, XA_HEAD_DIM)
    stacked = None
    for layer in range(DEPTH):
        proj_s = _norm_proj(hs, ng, w_in_r, layer, 256)
        hs, *stacked = _sample_mixer(proj_s, hs, ck, cv, st_pool, st_conv, st_ssm, lw, fng, sel,
                                     stacked, layer, 8, seq_s)
    pool_s, conv_s, ssm_s = stacked

    mem_shape = (DEPTH, batch, N_MEM, XA_HEADS, XA_HEAD_DIM)
    return (y_p.reshape(batch, seq_p, D_MODEL), hs.reshape(dec_batch, seq_s, D_MODEL),
            pool_p, conv_p, ssm_p.reshape(DEPTH, batch, SSD_HEADS, SSD_HEAD_DIM, D_STATE),
            k_flat.reshape(mem_shape), v_flat.reshape(mem_shape),
            pool_s, conv_s, ssm_s.reshape(DEPTH, dec_batch, SSD_HEADS, SSD_HEAD_DIM, D_STATE))
```

```python
import functools
import math

import jax
import jax.numpy as jnp
from jax import lax
from jax.experimental import pallas as pl
from jax.experimental.pallas import tpu as pltpu

F32 = jnp.float32
BF16 = jnp.bfloat16

D_MODEL = 1024
DEPTH = 4
PAST_LEN = 16384
POOL_WIDTH = 512
POOL_WINDOWS = (2, 4, 8, 16)
POOL_GROUP_DIM = 128
POOL_STATE_LEN = 15
SSD_WIDTH = 1024
SSD_HEAD_DIM = 64
SSD_HEADS = 16
SSD_GROUPS = 2
SSD_HEADS_PER_GROUP = 8
SSD_GROUP_WIDTH = SSD_WIDTH // SSD_GROUPS
D_STATE = 128
CONV_WIDTH = 4
CONV_DIM = 1536
XA_WIDTH = 512
XA_HEADS = 4
XA_HEAD_DIM = 128
N_MEM = 256
D_MIX = 2048
EPS = 1e-6
LOG2E = 1.4426950408889634

LANES = 128
SUBLANES = 8
POOL_HIST = 32
CONV_HIST = 8
CHUNK = 128
PROMPT_STEP = 256

COL_U = 0
COL_GP = COL_U + POOL_WIDTH
COL_Z = COL_GP + POOL_WIDTH
COL_XBC = COL_Z + SSD_WIDTH
COL_Q = COL_XBC + CONV_DIM
COL_GX = COL_Q + XA_WIDTH
COL_DT = COL_GX + XA_WIDTH
DT_COPIES = LANES // SSD_HEADS
PROJ_WIDTH = COL_DT + 2 * LANES

EXPAND_GROUPS = (0, 0, 1, 1, 2, 2, 0, 1)
EXPAND_TERM = (0, 1, 0, 1, 0, 1, 2, 2)
N_EXPAND = 3

VMEM_LIMIT_BYTES = 60 * 1024 * 1024


def _silu(x):
    h = 0.5 * x
    return h + h * jnp.tanh(h)


def _softplus(x):
    return jnp.maximum(x, 0.0) + jnp.log1p(jnp.exp(-jnp.abs(x)))


def _bf16_terms(v, n):
    terms = []
    for _ in range(n):
        t = v.astype(BF16).astype(F32)
        terms.append(t)
        v = v - t
    return terms


def _dot(a, b):
    return jnp.dot(a, b, preferred_element_type=F32)


def _dot_nt(a, b):
    return lax.dot_general(a, b, (((1,), (1,)), ((), ())), preferred_element_type=F32)


def _dot_tn(a, b):
    return lax.dot_general(a, b, (((0,), (0,)), ((), ())), preferred_element_type=F32)


def _exact_left_dot(sel_bf16, v):
    return sum(_dot(sel_bf16, t.astype(BF16)) for t in _bf16_terms(v, 3))


def _rms_scale(x, g):
    return x * lax.rsqrt(jnp.mean(x * x, axis=-1, keepdims=True) + EPS) * g


def _init_history(extp_ref, extc_ref, ssm_ref, init_refs):
    nb = extp_ref.shape[0]
    extp_ref[:, 0:POOL_HIST, :] = jnp.zeros((nb, POOL_HIST, POOL_WIDTH), F32)
    if init_refs is None:
        extc_ref[:, 0:CONV_HIST, :] = jnp.zeros((nb, CONV_HIST, CONV_DIM), F32)
        ssm_ref[...] = jnp.zeros(ssm_ref.shape, F32)
    else:
        pool0_ref, conv0_ref, ssm0_ref = init_refs
        for r in range(POOL_STATE_LEN):
            extp_ref[:, POOL_HIST - POOL_STATE_LEN + r, :] = pool0_ref[r]
        for r in range(CONV_WIDTH - 1):
            extc_ref[:, CONV_HIST - (CONV_WIDTH - 1) + r, :] = conv0_ref[r]
        ssm_ref[...] = ssm0_ref[...]


def _pool_heads(proj_ref, extp_ref, pa_ref, pb_ref, mix_ref, poolw_ref, pscale_ref, nb, seq, pos0):
    t = nb * seq
    n = POOL_HIST + seq
    extp_ref[:, POOL_HIST:n, :] = proj_ref[:, COL_U:COL_U + POOL_WIDTH].reshape(nb, seq, POOL_WIDTH)
    g = POOL_GROUP_DIM
    pa_ref[:, 8:n, :] = extp_ref[:, 8:n, :] + extp_ref[:, 7:n - 1, :]
    pb_ref[:, 16:n, g:4 * g] = pa_ref[:, 16:n, g:4 * g] + pa_ref[:, 14:n - 2, g:4 * g]
    pa_ref[:, 24:n, 2 * g:4 * g] = pb_ref[:, 24:n, 2 * g:4 * g] + pb_ref[:, 20:n - 4, 2 * g:4 * g]
    pb_ref[:, 32:n, 3 * g:4 * g] = pa_ref[:, 32:n, 3 * g:4 * g] + pa_ref[:, 24:n - 8, 3 * g:4 * g]
    tok = lax.broadcasted_iota(jnp.int32, (nb, seq, LANES), 1)
    pos1 = pos0 + tok + 1
    for gi, w in enumerate(POOL_WINDOWS):
        sl = slice(gi * g, (gi + 1) * g)
        src = pa_ref if gi % 2 == 0 else pb_ref
        cnt = jnp.minimum(pos1, w).astype(F32)
        d = (src[:, POOL_HIST:n, sl] / cnt - extp_ref[:, POOL_HIST:n, sl]).reshape(t, g)
        o = _dot(d.astype(BF16), poolw_ref[gi]) * pscale_ref[:, sl]
        gp = proj_ref[:, COL_GP + gi * g:COL_GP + (gi + 1) * g]
        mix_ref[:, sl] = (o * _silu(gp)).astype(BF16)


def _conv_silu(proj_ref, extc_ref, xc_ref, convw_ref, convb_ref, nb, seq):
    t = nb * seq
    extc_ref[:, CONV_HIST:CONV_HIST + seq, :] = proj_ref[:, COL_XBC:COL_XBC + CONV_DIM].reshape(
        nb, seq, CONV_DIM)
    cblk = 256
    for j in range(CONV_DIM // cblk):
        sl = slice(j * cblk, (j + 1) * cblk)
        if nb == 1:
            ext = extc_ref[0, :, sl]
            acc = convw_ref[0:1, sl] * ext
            for k in range(1, CONV_WIDTH):
                acc = pltpu.roll(acc, 1, 0) + convw_ref[k:k + 1, sl] * ext
            acc = acc[CONV_HIST:CONV_HIST + seq] + convb_ref[:, sl]
        else:
            acc = convb_ref[:, sl][None]
            for k in range(CONV_WIDTH):
                r0 = CONV_HIST - (CONV_WIDTH - 1) + k
                acc = acc + convw_ref[k:k + 1, sl][None] * extc_ref[:, r0:r0 + seq, sl]
        xc_ref[:, sl] = _silu(acc).reshape(t, cblk)


def _expand_heads(vals, sel_ref):
    t = vals[0].shape[0]
    terms = [_bf16_terms(v, 1 + max(tm for gq, tm in zip(EXPAND_GROUPS, EXPAND_TERM) if gq == i))
             for i, v in enumerate(vals)]
    group = lax.shift_right_logical(lax.broadcasted_iota(jnp.int32, (t, LANES), 1), 4)
    packed = terms[EXPAND_GROUPS[0]][EXPAND_TERM[0]]
    for gidx in range(1, len(EXPAND_GROUPS)):
        packed = jnp.where(group == gidx, terms[EXPAND_GROUPS[gidx]][EXPAND_TERM[gidx]], packed)
    out = _dot(packed.astype(BF16), sel_ref[...])
    return [out[:, i * SSD_WIDTH:(i + 1) * SSD_WIDTH] for i in range(len(vals))]


def _ssd_chunk(proj_ref, xc_ref, y_ref, ssm_ref, dtb_ref, arow_ref, sel_ref, nb, seq):
    t = nb * seq
    dt = _softplus(proj_ref[:, COL_DT:COL_DT + LANES] + dtb_ref[...])
    a = dt * arow_ref[...]
    ri = lax.broadcasted_iota(jnp.int32, (t, t), 0)
    ci = lax.broadcasted_iota(jnp.int32, (t, t), 1)
    if nb > 1:
        seq_start = ri - lax.rem(ri, seq)
        causal = jnp.bitwise_or(ci - seq_start, ri - ci) >= 0
    else:
        causal = ci <= ri
    tri = jnp.where(causal, 1.0, 0.0).astype(BF16)
    cum = _exact_left_dot(tri, a)
    cum2 = cum * LOG2E
    pad = LANES - t
    if pad:
        cum2_t = jnp.concatenate([cum2, jnp.zeros((pad, LANES), F32)], axis=0).T[:, 0:t]
    else:
        cum2_t = cum2.T
    cum3 = cum.reshape(nb, seq, LANES)
    cum_last = cum3[:, seq - 1:seq, :]
    e_tok = jnp.exp(cum)
    w_tok = (jnp.exp(cum_last - cum3) * dt.reshape(nb, seq, LANES)).reshape(t, LANES)
    dec = jnp.exp(cum_last)
    e_exp, w_exp, dt_exp = _expand_heads([e_tok, w_tok, dt], sel_ref)

    lane = lax.broadcasted_iota(jnp.int32, (t, LANES), 1)
    low_half = lane < SSD_HEAD_DIM

    for g in range(SSD_GROUPS):
        b0 = SSD_WIDTH + g * D_STATE
        c0 = SSD_WIDTH + SSD_GROUPS * D_STATE + g * D_STATE
        bg = xc_ref[:, b0:b0 + D_STATE].astype(BF16)
        cg = xc_ref[:, c0:c0 + D_STATE].astype(BF16)
        cb = jnp.where(causal, _dot_nt(cg, bg), 0.0)
        for pp in range(SSD_HEADS_PER_GROUP // 2):
            h0 = g * SSD_HEADS_PER_GROUP + 2 * pp
            ms = []
            for hh in (h0, h0 + 1):
                seg2 = cum2[:, hh:hh + 1] - cum2_t[hh:hh + 1, :]
                ms.append((cb * jnp.exp2(jnp.minimum(seg2, 0.0))).astype(BF16))
            col = (h0 // 2) * LANES
            xdt = xc_ref[:, col:col + LANES] * dt_exp[:, col:col + LANES]
            rhs = jnp.concatenate(
                [jnp.where(low_half, xdt, 0.0), jnp.where(low_half, 0.0, xdt)], axis=0).astype(BF16)
            y_ref[:, col:col + LANES] = _dot(jnp.concatenate(ms, axis=1), rhs)
        gsl = slice(g * SSD_GROUP_WIDTH, (g + 1) * SSD_GROUP_WIDTH)
        for b in range(nb):
            rows = slice(b * seq, (b + 1) * seq)
            s_old = ssm_ref[b, gsl, :]
            yi = _dot_nt(cg[rows], s_old.astype(BF16))
            y_ref[rows, gsl] = y_ref[rows, gsl] + yi * e_exp[rows, gsl]
            xw = (xc_ref[rows, gsl] * w_exp[rows, gsl]).astype(BF16)
            upd = _dot_tn(xw, bg[rows])
            for r in range(SSD_HEADS_PER_GROUP):
                hh = g * SSD_HEADS_PER_GROUP + r
                hs = slice(r * SSD_HEAD_DIM, (r + 1) * SSD_HEAD_DIM)
                r0 = g * SSD_GROUP_WIDTH + r * SSD_HEAD_DIM
                ssm_ref[b, r0:r0 + SSD_HEAD_DIM, :] = s_old[hs] * dec[b, :, hh:hh + 1] + upd[hs]


def _gated_group_norm(proj_ref, xc_ref, y_ref, mix_ref, dskip_ref, normg_ref):
    z = proj_ref[:, COL_Z:COL_Z + SSD_WIDTH]
    y = (y_ref[...] + dskip_ref[...] * xc_ref[:, 0:SSD_WIDTH]) * _silu(z)
    for g in range(SSD_GROUPS):
        gsl = slice(g * SSD_GROUP_WIDTH, (g + 1) * SSD_GROUP_WIDTH)
        mix_ref[:, POOL_WIDTH + g * SSD_GROUP_WIDTH:POOL_WIDTH + (g + 1) * SSD_GROUP_WIDTH] = (
            _rms_scale(y[:, gsl], normg_ref[:, gsl])).astype(BF16)


def _softmax_rows(s):
    s = s - jnp.max(s, axis=-1, keepdims=True)
    p = jnp.exp(s)
    return p / jnp.sum(p, axis=-1, keepdims=True)


def _xattn_dense(proj_ref, mix_ref, k_ref, v_ref):
    inv_sqrt_d = 1.0 / math.sqrt(XA_HEAD_DIM)
    for hh in range(XA_HEADS):
        sl = slice(hh * XA_HEAD_DIM, (hh + 1) * XA_HEAD_DIM)
        q = proj_ref[:, COL_Q + hh * XA_HEAD_DIM:COL_Q + (hh + 1) * XA_HEAD_DIM].astype(BF16)
        p = _softmax_rows(_dot_nt(q, k_ref[:, sl]) * inv_sqrt_d)
        o = _dot(p.astype(BF16), v_ref[:, sl])
        gx = proj_ref[:, COL_GX + hh * XA_HEAD_DIM:COL_GX + (hh + 1) * XA_HEAD_DIM]
        m0 = POOL_WIDTH + SSD_WIDTH + hh * XA_HEAD_DIM
        mix_ref[:, m0:m0 + XA_HEAD_DIM] = (o * _silu(gx)).astype(BF16)


def _xattn_rowmajor(proj_ref, mix_ref, k_ref, v_ref, nb, seq):
    t = nb * seq
    inv_sqrt_d = 1.0 / math.sqrt(XA_HEAD_DIM)
    q4 = jnp.concatenate(
        [proj_ref[:, COL_Q + hh * XA_HEAD_DIM:COL_Q + (hh + 1) * XA_HEAD_DIM].astype(BF16).reshape(
            nb, seq, XA_HEAD_DIM) for hh in range(XA_HEADS)], axis=1)
    kf = k_ref[...].astype(BF16)
    vf = v_ref[...].astype(BF16)
    s = lax.dot_general(q4, kf, (((2,), (2,)), ((0,), (0,))), preferred_element_type=F32) * inv_sqrt_d
    shape = (nb, XA_HEADS * seq, N_MEM * XA_HEADS)
    row = lax.broadcasted_iota(jnp.int32, shape, 1)
    col = lax.broadcasted_iota(jnp.int32, shape, 2)
    own_head = jnp.bitwise_and(col, XA_HEADS - 1) * seq
    valid = jnp.bitwise_or(row - own_head, own_head + (seq - 1) - row) >= 0
    p = _softmax_rows(jnp.where(valid, s, -1e30))
    o4 = lax.dot_general(p.astype(BF16), vf, (((2,), (1,)), ((0,), (0,))), preferred_element_type=F32)
    for hh in range(XA_HEADS):
        o = o4[:, hh * seq:(hh + 1) * seq, :].reshape(t, XA_HEAD_DIM)
        gx = proj_ref[:, COL_GX + hh * XA_HEAD_DIM:COL_GX + (hh + 1) * XA_HEAD_DIM]
        m0 = POOL_WIDTH + SSD_WIDTH + hh * XA_HEAD_DIM
        mix_ref[:, m0:m0 + XA_HEAD_DIM] = (o * _silu(gx)).astype(BF16)


def _mem_kv_kernel(mem_ref, g_ref, w_ref, kf_ref, vf_ref, kd_ref, vd_ref):
    xn = _rms_scale(mem_ref[...], g_ref[...]).astype(BF16)
    kv = _dot(xn, w_ref[...])
    kd_ref[...] = kv[:, 0:XA_WIDTH].astype(BF16)
    vd_ref[...] = kv[:, XA_WIDTH:2 * XA_WIDTH].astype(BF16)
    for hh in range(XA_HEADS):
        kf_ref[pl.ds(hh, N_MEM, stride=XA_HEADS), :] = kv[:, hh * XA_HEAD_DIM:(hh + 1) * XA_HEAD_DIM]
        vf_ref[pl.ds(hh, N_MEM, stride=XA_HEADS), :] = kv[
            :, XA_WIDTH + hh * XA_HEAD_DIM:XA_WIDTH + (hh + 1) * XA_HEAD_DIM]


def _mem_kv(mem2d, mem_norm_g, w_kv):
    batch = mem2d.shape[0] // N_MEM
    flat = jax.ShapeDtypeStruct((DEPTH, batch, N_MEM * XA_HEADS, XA_HEAD_DIM), F32)
    dense = jax.ShapeDtypeStruct((DEPTH, batch, N_MEM, XA_WIDTH), BF16)
    flat_spec = pl.BlockSpec((None, None, N_MEM * XA_HEADS, XA_HEAD_DIM), lambda l, b: (l, b, 0, 0))
    dense_spec = pl.BlockSpec((None, None, N_MEM, XA_WIDTH), lambda l, b: (l, b, 0, 0))
    return pl.pallas_call(
        _mem_kv_kernel,
        grid=(DEPTH, batch),
        in_specs=[
            pl.BlockSpec((N_MEM, D_MODEL), lambda l, b: (b, 0)),
            pl.BlockSpec((None, 1, D_MODEL), lambda l, b: (l, 0, 0)),
            pl.BlockSpec((None, D_MODEL, 2 * XA_WIDTH), lambda l, b: (l, 0, 0)),
        ],
        out_specs=[flat_spec, flat_spec, dense_spec, dense_spec],
        out_shape=[flat, flat, dense, dense],
        compiler_params=pltpu.CompilerParams(dimension_semantics=("arbitrary", "arbitrary")),
        name="mem_kv",
    )(mem2d, mem_norm_g.reshape(DEPTH, 1, D_MODEL), w_kv)


def _prompt_kernel(x_ref, k_ref, v_ref, ng_ref, win_ref, wout_ref, poolw_ref, pscale_ref, convw_ref,
                   convb_ref, dtb_ref, arow_ref, dskip_ref, normg_ref, fng_ref, sel_ref,
                   y_out_ref, poolo_ref, convo_ref, ssm_ref,
                   h_ref, proj_ref, extp_ref, pa_ref, pb_ref, extc_ref, xc_ref, y_ref, mix_ref):
    layer = pl.program_id(1)
    c = pl.program_id(2)
    rows = pl.ds(pl.multiple_of(c * PROMPT_STEP, PROMPT_STEP), PROMPT_STEP)

    @pl.when(layer == 0)
    def _load():
        h_ref[rows, :] = x_ref[...]

    @pl.when(c == 0)
    def _init():
        _init_history(extp_ref, extc_ref, ssm_ref, None)

    h = h_ref[rows, :]
    proj_ref[...] = _dot(_rms_scale(h, ng_ref[...]).astype(BF16), win_ref[...])

    _pool_heads(proj_ref, extp_ref, pa_ref, pb_ref, mix_ref, poolw_ref, pscale_ref, 1, PROMPT_STEP,
                c * PROMPT_STEP)
    poolo_ref[...] = extp_ref[:, PROMPT_STEP + POOL_HIST - POOL_STATE_LEN:PROMPT_STEP + POOL_HIST, :]
    extp_ref[:, 0:POOL_HIST, :] = extp_ref[:, PROMPT_STEP:PROMPT_STEP + POOL_HIST, :]

    _conv_silu(proj_ref, extc_ref, xc_ref, convw_ref, convb_ref, 1, PROMPT_STEP)
    convo_ref[...] = extc_ref[:, PROMPT_STEP + CONV_HIST - (CONV_WIDTH - 1):PROMPT_STEP + CONV_HIST, :]
    extc_ref[:, CONV_HIST - (CONV_WIDTH - 1):CONV_HIST, :] = extc_ref[
        :, PROMPT_STEP + CONV_HIST - (CONV_WIDTH - 1):PROMPT_STEP + CONV_HIST, :]

    for s in range(PROMPT_STEP // CHUNK):
        sub = pl.ds(s * CHUNK, CHUNK)
        _ssd_chunk(proj_ref.at[sub], xc_ref.at[sub], y_ref.at[sub], ssm_ref, dtb_ref, arow_ref,
                   sel_ref, 1, CHUNK)
    _gated_group_norm(proj_ref, xc_ref, y_ref, mix_ref, dskip_ref, normg_ref)
    _xattn_dense(proj_ref, mix_ref, k_ref, v_ref)

    h_new = h + _dot(mix_ref[...], wout_ref[...])
    h_ref[rows, :] = h_new

    @pl.when(layer == DEPTH - 1)
    def _final():
        y_out_ref[...] = _rms_scale(h_new, fng_ref[...])


def _prompt_stack(x2d, k_dense, v_dense, ng, w_in_r, lw, fng, sel, batch, seq_len):
    n_steps = seq_len // PROMPT_STEP
    assert n_steps * PROMPT_STEP == seq_len
    last = DEPTH - 1

    def lay3(shape):
        return pl.BlockSpec((None,) + shape, lambda b, l, c: (l, 0, 0))

    def per_seq(shape):
        return pl.BlockSpec((None, None) + shape, lambda b, l, c: (l, b, 0, 0))

    def state_out(shape):
        return pl.BlockSpec((None, 1) + shape, lambda b, l, c: (l, b, 0, 0))

    x_spec = pl.BlockSpec((PROMPT_STEP, D_MODEL),
                          lambda b, l, c: (b * n_steps + jnp.where(l == 0, c, n_steps - 1), 0))
    y_spec = pl.BlockSpec((PROMPT_STEP, D_MODEL),
                          lambda b, l, c: (b * n_steps + jnp.where(l == last, c, 0), 0))
    in_specs = [
        x_spec,
        per_seq((N_MEM, XA_WIDTH)), per_seq((N_MEM, XA_WIDTH)),
        lay3((1, D_MODEL)),
        lay3((D_MODEL, PROJ_WIDTH)),
        lay3((D_MIX, D_MODEL)),
        pl.BlockSpec((None, 4, POOL_GROUP_DIM, POOL_GROUP_DIM), lambda b, l, c: (l, 0, 0, 0)),
        lay3((1, POOL_WIDTH)),
        lay3((CONV_WIDTH, CONV_DIM)),
        lay3((1, CONV_DIM)),
        lay3((1, LANES)),
        lay3((1, LANES)),
        lay3((1, SSD_WIDTH)),
        lay3((1, SSD_WIDTH)),
        pl.BlockSpec((1, D_MODEL), lambda b, l, c: (0, 0)),
        pl.BlockSpec((LANES, N_EXPAND * SSD_WIDTH), lambda b, l, c: (0, 0)),
    ]
    out_specs = [
        y_spec,
        state_out((POOL_STATE_LEN, POOL_WIDTH)),
        state_out((CONV_WIDTH - 1, CONV_DIM)),
        state_out((SSD_WIDTH, D_STATE)),
    ]
    out_shape = [
        jax.ShapeDtypeStruct((batch * seq_len, D_MODEL), F32),
        jax.ShapeDtypeStruct((DEPTH, batch, POOL_STATE_LEN, POOL_WIDTH), F32),
        jax.ShapeDtypeStruct((DEPTH, batch, CONV_WIDTH - 1, CONV_DIM), F32),
        jax.ShapeDtypeStruct((DEPTH, batch, SSD_WIDTH, D_STATE), F32),
    ]
    scratch = [
        pltpu.VMEM((seq_len, D_MODEL), F32),
        pltpu.VMEM((PROMPT_STEP, PROJ_WIDTH), F32),
        pltpu.VMEM((1, POOL_HIST + PROMPT_STEP, POOL_WIDTH), F32),
        pltpu.VMEM((1, POOL_HIST + PROMPT_STEP, POOL_WIDTH), F32),
        pltpu.VMEM((1, POOL_HIST + PROMPT_STEP, POOL_WIDTH), F32),
        pltpu.VMEM((1, CONV_HIST + PROMPT_STEP, CONV_DIM), F32),
        pltpu.VMEM((PROMPT_STEP, CONV_DIM), F32),
        pltpu.VMEM((PROMPT_STEP, SSD_WIDTH), F32),
        pltpu.VMEM((PROMPT_STEP, D_MIX), BF16),
    ]
    return pl.pallas_call(
        _prompt_kernel,
        grid=(batch, DEPTH, n_steps),
        in_specs=in_specs,
        out_specs=out_specs,
        out_shape=out_shape,
        scratch_shapes=scratch,
        compiler_params=pltpu.CompilerParams(
            dimension_semantics=("arbitrary", "arbitrary", "arbitrary"),
            vmem_limit_bytes=VMEM_LIMIT_BYTES),
        name="prompt_stack",
    )(x2d, k_dense, v_dense, ng, w_in_r, *lw, fng, sel)


def _sample_kernel(*refs, nb, seq, n_alias, final):
    (h_ref, k_ref, v_ref, pool0_ref, conv0_ref, ssm0_ref, ng_ref, win_ref,
     wout_ref, poolw_ref, pscale_ref, convw_ref, convb_ref, dtb_ref, arow_ref, dskip_ref,
     normg_ref, fng_ref, sel_ref) = refs[:19]
    (hout_ref, poolo_ref, convo_ref, ssm_ref,
     proj_ref, extp_ref, pa_ref, pb_ref, extc_ref, xc_ref, y_ref, mix_ref) = refs[19 + n_alias:]

    proj_ref[...] = _dot(_rms_scale(h_ref[...], ng_ref[...]).astype(BF16), win_ref[...])
    _init_history(extp_ref, extc_ref, ssm_ref, (pool0_ref, conv0_ref, ssm0_ref))
    _pool_heads(proj_ref, extp_ref, pa_ref, pb_ref, mix_ref, poolw_ref, pscale_ref, nb, seq, PAST_LEN)
    poolo_ref[...] = extp_ref[:, seq + POOL_HIST - POOL_STATE_LEN:seq + POOL_HIST, :]
    _conv_silu(proj_ref, extc_ref, xc_ref, convw_ref, convb_ref, nb, seq)
    convo_ref[...] = extc_ref[:, seq + CONV_HIST - (CONV_WIDTH - 1):seq + CONV_HIST, :]
    _ssd_chunk(proj_ref, xc_ref, y_ref, ssm_ref, dtb_ref, arow_ref, sel_ref, nb, seq)
    _gated_group_norm(proj_ref, xc_ref, y_ref, mix_ref, dskip_ref, normg_ref)
    _xattn_rowmajor(proj_ref, mix_ref, k_ref, v_ref, nb, seq)
    h_new = h_ref[...] + _dot(mix_ref[...], wout_ref[...])
    hout_ref[...] = _rms_scale(h_new, fng_ref[...]) if final else h_new


def _sample_mixer(h, k_all, v_all, st_pool, st_conv, st_ssm, ng, w_in_r, lw, fng, sel, stacked, layer,
                  nb, seq):
    n_seq = h.shape[0] // seq
    t = nb * seq
    assert n_seq % nb == 0
    final = layer == DEPTH - 1

    def per_seq(shape):
        return pl.BlockSpec((None, nb) + shape, lambda i: (layer, i) + (0,) * len(shape))

    def lay(shape):
        return pl.BlockSpec((None,) + shape, lambda i: (layer,) + (0,) * len(shape))

    def rows_by_seq(n_rows, width):
        return pl.BlockSpec((None, n_rows, nb, width), lambda i: (layer, 0, i, 0))

    def resident(shape):
        return pl.BlockSpec((None,) + shape, lambda i: (layer,) + (0,) * len(shape),
                            pipeline_mode=pl.Buffered(1))

    rows = lambda width: pl.BlockSpec((t, width), lambda i: (i, 0))
    in_specs = [
        rows(D_MODEL),
        per_seq((N_MEM * XA_HEADS, XA_HEAD_DIM)), per_seq((N_MEM * XA_HEADS, XA_HEAD_DIM)),
        rows_by_seq(POOL_STATE_LEN, POOL_WIDTH), rows_by_seq(CONV_WIDTH - 1, CONV_DIM),
        per_seq((SSD_WIDTH, D_STATE)),
        lay((1, D_MODEL)), resident((D_MODEL, PROJ_WIDTH)),
        resident((D_MIX, D_MODEL)), lay((4, POOL_GROUP_DIM, POOL_GROUP_DIM)), lay((1, POOL_WIDTH)),
        lay((CONV_WIDTH, CONV_DIM)), lay((1, CONV_DIM)), lay((1, LANES)), lay((1, LANES)),
        lay((1, SSD_WIDTH)), lay((1, SSD_WIDTH)),
        pl.BlockSpec((1, D_MODEL), lambda i: (0, 0)),
        pl.BlockSpec((LANES, N_EXPAND * SSD_WIDTH), lambda i: (0, 0)),
    ]
    args = [h, k_all, v_all, st_pool, st_conv, st_ssm, ng, w_in_r, *lw, fng, sel]
    aliases = {}
    n_alias = 0
    if stacked is not None:
        n_alias = len(stacked)
        for j, arr in enumerate(stacked):
            aliases[len(args)] = 1 + j
            in_specs.append(pl.BlockSpec(memory_space=pl.ANY))
            args.append(arr)
    out_specs = [
        rows(D_MODEL),
        per_seq((POOL_STATE_LEN, POOL_WIDTH)), per_seq((CONV_WIDTH - 1, CONV_DIM)),
        per_seq((SSD_WIDTH, D_STATE)),
    ]
    out_shape = [
        jax.ShapeDtypeStruct(h.shape, F32),
        jax.ShapeDtypeStruct((DEPTH, n_seq, POOL_STATE_LEN, POOL_WIDTH), F32),
        jax.ShapeDtypeStruct((DEPTH, n_seq, CONV_WIDTH - 1, CONV_DIM), F32),
        jax.ShapeDtypeStruct((DEPTH, n_seq, SSD_WIDTH, D_STATE), F32),
    ]
    scratch = [
        pltpu.VMEM((t, PROJ_WIDTH), F32),
        pltpu.VMEM((nb, POOL_HIST + seq, POOL_WIDTH), F32),
        pltpu.VMEM((nb, POOL_HIST + seq, POOL_WIDTH), F32),
        pltpu.VMEM((nb, POOL_HIST + seq, POOL_WIDTH), F32),
        pltpu.VMEM((nb, CONV_HIST + seq, CONV_DIM), F32),
        pltpu.VMEM((t, CONV_DIM), F32),
        pltpu.VMEM((t, SSD_WIDTH), F32),
        pltpu.VMEM((t, D_MIX), BF16),
    ]
    return pl.pallas_call(
        functools.partial(_sample_kernel, nb=nb, seq=seq, n_alias=n_alias, final=final),
        grid=(n_seq // nb,),
        in_specs=in_specs,
        out_specs=out_specs,
        out_shape=out_shape,
        scratch_shapes=scratch,
        input_output_aliases=aliases,
        compiler_params=pltpu.CompilerParams(
            dimension_semantics=("arbitrary",), vmem_limit_bytes=VMEM_LIMIT_BYTES),
        name="sample_mixer",
    )(*args)


REORDER_BLOCK = 256
N_ALIGNED_BLOCKS = COL_Q // REORDER_BLOCK
N_MAIN_BLOCKS = COL_DT // REORDER_BLOCK


def _reorder_w_in_kernel(wt_ref, o_ref):
    j = pl.program_id(1)
    t = wt_ref[0].T.astype(BF16)

    @pl.when(j < N_MAIN_BLOCKS)
    def _main():
        o_ref[...] = t

    @pl.when(j == N_MAIN_BLOCKS)
    def _dt():
        o_ref[:, 0:LANES] = jnp.concatenate([t[:, 0:SSD_HEADS]] * DT_COPIES, axis=-1)
        o_ref[:, LANES:REORDER_BLOCK] = jnp.zeros((D_MODEL, REORDER_BLOCK - LANES), BF16)


def _reorder_w_in(w_in):
    depth, d, n = w_in.shape
    src_dt = COL_Q
    wt = jnp.swapaxes(w_in, 1, 2)

    def src_row(l, j):
        shifted = REORDER_BLOCK * j + jnp.where(j >= N_ALIGNED_BLOCKS, SSD_HEADS, 0)
        return (l, pl.multiple_of(jnp.where(j == N_MAIN_BLOCKS, src_dt, shifted), SSD_HEADS), 0)

    return pl.pallas_call(
        _reorder_w_in_kernel,
        grid=(depth, N_MAIN_BLOCKS + 1),
        in_specs=[pl.BlockSpec((pl.Element(1), pl.Element(REORDER_BLOCK), pl.Element(d)), src_row)],
        out_specs=pl.BlockSpec((None, d, REORDER_BLOCK), lambda l, j: (l, 0, j)),
        out_shape=jax.ShapeDtypeStruct((depth, d, PROJ_WIDTH), BF16),
        compiler_params=pltpu.CompilerParams(dimension_semantics=("arbitrary", "arbitrary")),
        name="reorder_w_in",
    )(wt)


def _expand_selector():
    lane = jnp.arange(LANES, dtype=jnp.int32)
    quantity = jnp.asarray(EXPAND_GROUPS, jnp.int32)[lane // SSD_HEADS]
    head = lane % SSD_HEADS
    col = jnp.arange(N_EXPAND * SSD_WIDTH, dtype=jnp.int32)
    col_quantity = col // SSD_WIDTH
    col_head = (col % SSD_WIDTH) // SSD_HEAD_DIM
    return jnp.logical_and(quantity[:, None] == col_quantity[None, :],
                           head[:, None] == col_head[None, :]).astype(BF16)


def kernel(x_prompt, x_sample, mem_prompt, state_pool, state_conv, state_ssm, cache_mem_k, cache_mem_v,
           norm_g, w_in, pool_w, pool_scale, conv_w, conv_b, dt_bias, a_log, d_skip, ssd_norm_g,
           mem_norm_g, w_mem_k, w_mem_v, w_out, final_norm_g):
    batch, seq_p, _ = x_prompt.shape
    dec_batch, seq_s, _ = x_sample.shape

    w_in_r = _reorder_w_in(w_in)
    w_kv = jnp.concatenate([w_mem_k, w_mem_v], axis=-1).astype(BF16)
    sel = _expand_selector()
    fng = final_norm_g.reshape(1, D_MODEL)
    ng = norm_g.reshape(DEPTH, 1, D_MODEL)
    lw = (w_out.astype(BF16), pool_w.astype(BF16), pool_scale.reshape(DEPTH, 1, POOL_WIDTH), conv_w,
          conv_b.reshape(DEPTH, 1, CONV_DIM),
          jnp.tile(dt_bias, (1, DT_COPIES)).reshape(DEPTH, 1, LANES),
          jnp.tile(-jnp.exp(a_log), (1, DT_COPIES)).reshape(DEPTH, 1, LANES),
          jnp.repeat(d_skip, SSD_HEAD_DIM, axis=-1).reshape(DEPTH, 1, SSD_WIDTH),
          ssd_norm_g.reshape(DEPTH, 1, SSD_WIDTH))

    k_flat, v_flat, k_dense, v_dense = _mem_kv(
        mem_prompt.reshape(batch * N_MEM, D_MODEL), mem_norm_g, w_kv)
    y_p, pool_p, conv_p, ssm_p = _prompt_stack(
        x_prompt.reshape(batch * seq_p, D_MODEL), k_dense, v_dense, ng, w_in_r, lw, fng, sel,
        batch, seq_p)

    hs = x_sample.reshape(dec_batch * seq_s, D_MODEL)
    st_ssm = state_ssm.reshape(DEPTH, dec_batch, SSD_WIDTH, D_STATE)
    st_pool = jnp.swapaxes(state_pool, 1, 2)
    st_conv = jnp.swapaxes(state_conv, 1, 2)
    ck = cache_mem_k.reshape(DEPTH, dec_batch, N_MEM * XA_HEADS, XA_HEAD_DIM)
    cv = cache_mem_v.reshape(DEPTH, dec_batch, N_MEM * XA_HEADS, XA_HEAD_DIM)
    stacked = None
    for layer in range(DEPTH):
        hs, *stacked = _sample_mixer(hs, ck, cv, st_pool, st_conv, st_ssm, ng, w_in_r, lw, fng, sel,
                                     stacked, layer, 8, seq_s)
    pool_s, conv_s, ssm_s = stacked

    mem_shape = (DEPTH, batch, N_MEM, XA_HEADS, XA_HEAD_DIM)
    return (y_p.reshape(batch, seq_p, D_MODEL), hs.reshape(dec_batch, seq_s, D_MODEL),
            pool_p, conv_p, ssm_p.reshape(DEPTH, batch, SSD_HEADS, SSD_HEAD_DIM, D_STATE),
            k_flat.reshape(mem_shape), v_flat.reshape(mem_shape),
            pool_s, conv_s, ssm_s.reshape(DEPTH, dec_batch, SSD_HEADS, SSD_HEAD_DIM, D_STATE))
```

```python
import functools
import math

import jax
import jax.numpy as jnp
from jax import lax
from jax.experimental import pallas as pl
from jax.experimental.pallas import tpu as pltpu

F32 = jnp.float32
BF16 = jnp.bfloat16

D_MODEL = 1024
DEPTH = 4
PAST_LEN = 16384
POOL_WIDTH = 512
POOL_WINDOWS = (2, 4, 8, 16)
POOL_GROUP_DIM = 128
POOL_STATE_LEN = 15
SSD_WIDTH = 1024
SSD_HEAD_DIM = 64
SSD_HEADS = 16
SSD_GROUPS = 2
SSD_HEADS_PER_GROUP = 8
SSD_GROUP_WIDTH = SSD_WIDTH // SSD_GROUPS
D_STATE = 128
CONV_WIDTH = 4
CONV_DIM = 1536
XA_WIDTH = 512
XA_HEADS = 4
XA_HEAD_DIM = 128
N_MEM = 256
D_MIX = 2048
EPS = 1e-6
LOG2E = 1.4426950408889634

LANES = 128
SUBLANES = 8
POOL_HIST = 32
CONV_HIST = 8
CHUNK = 128
PROMPT_STEP = 256

COL_U = 0
COL_GP = COL_U + POOL_WIDTH
COL_Z = COL_GP + POOL_WIDTH
COL_XBC = COL_Z + SSD_WIDTH
COL_Q = COL_XBC + CONV_DIM
COL_GX = COL_Q + XA_WIDTH
COL_DT = COL_GX + XA_WIDTH
DT_COPIES = LANES // SSD_HEADS
PROJ_WIDTH = COL_DT + 2 * LANES

EXPAND_GROUPS = (0, 0, 1, 1, 2, 2, 0, 1)
EXPAND_TERM = (0, 1, 0, 1, 0, 1, 2, 2)
N_EXPAND = 3

VMEM_LIMIT_BYTES = 60 * 1024 * 1024


def _silu(x):
    h = 0.5 * x
    return h + h * jnp.tanh(h)


def _softplus(x):
    return jnp.maximum(x, 0.0) + jnp.log1p(jnp.exp(-jnp.abs(x)))


def _bf16_terms(v, n):
    terms = []
    for _ in range(n):
        t = v.astype(BF16).astype(F32)
        terms.append(t)
        v = v - t
    return terms


def _dot(a, b):
    return jnp.dot(a, b, preferred_element_type=F32)


def _dot_nt(a, b):
    return lax.dot_general(a, b, (((1,), (1,)), ((), ())), preferred_element_type=F32)


def _dot_tn(a, b):
    return lax.dot_general(a, b, (((0,), (0,)), ((), ())), preferred_element_type=F32)


def _exact_left_dot(sel_bf16, v):
    return sum(_dot(sel_bf16, t.astype(BF16)) for t in _bf16_terms(v, 3))


def _rms_scale(x, g):
    return x * lax.rsqrt(jnp.mean(x * x, axis=-1, keepdims=True) + EPS) * g


def _init_history(extp_ref, extc_ref, ssm_ref, init_refs):
    nb = extp_ref.shape[0]
    extp_ref[:, 0:POOL_HIST, :] = jnp.zeros((nb, POOL_HIST, POOL_WIDTH), F32)
    extc_ref[:, 0:CONV_HIST, :] = jnp.zeros((nb, CONV_HIST, CONV_DIM), F32)
    if init_refs is None:
        ssm_ref[...] = jnp.zeros(ssm_ref.shape, F32)
    else:
        pool0_ref, conv0_ref = init_refs
        for r in range(POOL_STATE_LEN):
            extp_ref[:, POOL_HIST - POOL_STATE_LEN + r, :] = pool0_ref[r]
        for r in range(CONV_WIDTH - 1):
            extc_ref[:, CONV_HIST - (CONV_WIDTH - 1) + r, :] = conv0_ref[r]


def _pool_heads(proj_ref, extp_ref, pa_ref, pb_ref, mix_ref, poolw_ref, pscale_ref, nb, seq, pos0):
    t = nb * seq
    n = POOL_HIST + seq
    extp_ref[:, POOL_HIST:n, :] = proj_ref[:, COL_U:COL_U + POOL_WIDTH].reshape(nb, seq, POOL_WIDTH)
    g = POOL_GROUP_DIM
    pa_ref[:, 8:n, :] = extp_ref[:, 8:n, :] + extp_ref[:, 7:n - 1, :]
    pb_ref[:, 16:n, g:4 * g] = pa_ref[:, 16:n, g:4 * g] + pa_ref[:, 14:n - 2, g:4 * g]
    pa_ref[:, 24:n, 2 * g:4 * g] = pb_ref[:, 24:n, 2 * g:4 * g] + pb_ref[:, 20:n - 4, 2 * g:4 * g]
    pb_ref[:, 32:n, 3 * g:4 * g] = pa_ref[:, 32:n, 3 * g:4 * g] + pa_ref[:, 24:n - 8, 3 * g:4 * g]
    tok = lax.broadcasted_iota(jnp.int32, (nb, seq, LANES), 1)
    pos1 = pos0 + tok + 1
    for gi, w in enumerate(POOL_WINDOWS):
        sl = slice(gi * g, (gi + 1) * g)
        src = pa_ref if gi % 2 == 0 else pb_ref
        cnt = jnp.minimum(pos1, w).astype(F32)
        d = (src[:, POOL_HIST:n, sl] / cnt - extp_ref[:, POOL_HIST:n, sl]).reshape(t, g)
        o = _dot(d.astype(BF16), poolw_ref[gi]) * pscale_ref[:, sl]
        gp = proj_ref[:, COL_GP + gi * g:COL_GP + (gi + 1) * g]
        mix_ref[:, sl] = (o * _silu(gp)).astype(BF16)


def _conv_silu(proj_ref, extc_ref, xc_ref, convw_ref, convb_ref, nb, seq):
    t = nb * seq
    extc_ref[:, CONV_HIST:CONV_HIST + seq, :] = proj_ref[:, COL_XBC:COL_XBC + CONV_DIM].reshape(
        nb, seq, CONV_DIM)
    cblk = 256
    for j in range(CONV_DIM // cblk):
        sl = slice(j * cblk, (j + 1) * cblk)
        ext = extc_ref[:, :, sl]
        acc = convw_ref[0:1, sl][None] * ext
        for k in range(1, CONV_WIDTH):
            acc = pltpu.roll(acc, 1, 1) + convw_ref[k:k + 1, sl][None] * ext
        acc = acc[:, CONV_HIST:CONV_HIST + seq, :] + convb_ref[:, sl][None]
        xc_ref[:, sl] = _silu(acc).reshape(t, cblk)


def _expand_heads(vals, sel_ref):
    t = vals[0].shape[0]
    terms = [_bf16_terms(v, 1 + max(tm for gq, tm in zip(EXPAND_GROUPS, EXPAND_TERM) if gq == i))
             for i, v in enumerate(vals)]
    group = lax.shift_right_logical(lax.broadcasted_iota(jnp.int32, (t, LANES), 1), 4)
    packed = terms[EXPAND_GROUPS[0]][EXPAND_TERM[0]]
    for gidx in range(1, len(EXPAND_GROUPS)):
        packed = jnp.where(group == gidx, terms[EXPAND_GROUPS[gidx]][EXPAND_TERM[gidx]], packed)
    out = _dot(packed.astype(BF16), sel_ref[...])
    return [out[:, i * SSD_WIDTH:(i + 1) * SSD_WIDTH] for i in range(len(vals))]


def _ssd_chunk(proj_ref, xc_ref, y_ref, ssm_in_ref, ssm_ref, dtb_ref, arow_ref, sel_ref, nb, seq):
    t = nb * seq
    dt = _softplus(proj_ref[:, COL_DT:COL_DT + LANES] + dtb_ref[...])
    a = dt * arow_ref[...]
    ri = lax.broadcasted_iota(jnp.int32, (t, t), 0)
    ci = lax.broadcasted_iota(jnp.int32, (t, t), 1)
    if nb > 1:
        seq_start = ri - lax.rem(ri, seq)
        causal = jnp.bitwise_or(ci - seq_start, ri - ci) >= 0
    else:
        causal = ci <= ri
    tri = jnp.where(causal, 1.0, 0.0).astype(BF16)
    cum = _exact_left_dot(tri, a)
    cum2 = cum * LOG2E
    pad = LANES - t
    if pad:
        cum2_t = jnp.concatenate([cum2, jnp.zeros((pad, LANES), F32)], axis=0).T[:, 0:t]
    else:
        cum2_t = cum2.T
    cum3 = cum.reshape(nb, seq, LANES)
    cum_last = cum3[:, seq - 1:seq, :]
    e_tok = jnp.exp(cum)
    w_tok = (jnp.exp(cum_last - cum3) * dt.reshape(nb, seq, LANES)).reshape(t, LANES)
    dec = jnp.exp(cum_last)
    e_exp, w_exp, dt_exp = _expand_heads([e_tok, w_tok, dt], sel_ref)

    lane = lax.broadcasted_iota(jnp.int32, (t, LANES), 1)
    low_half = lane < SSD_HEAD_DIM

    for g in range(SSD_GROUPS):
        b0 = SSD_WIDTH + g * D_STATE
        c0 = SSD_WIDTH + SSD_GROUPS * D_STATE + g * D_STATE
        bg = xc_ref[:, b0:b0 + D_STATE].astype(BF16)
        cg = xc_ref[:, c0:c0 + D_STATE].astype(BF16)
        cb = jnp.where(causal, _dot_nt(cg, bg), 0.0)
        for pp in range(SSD_HEADS_PER_GROUP // 2):
            h0 = g * SSD_HEADS_PER_GROUP + 2 * pp
            ms = []
            for hh in (h0, h0 + 1):
                seg2 = cum2[:, hh:hh + 1] - cum2_t[hh:hh + 1, :]
                ms.append((cb * jnp.exp2(jnp.minimum(seg2, 0.0))).astype(BF16))
            col = (h0 // 2) * LANES
            xdt = xc_ref[:, col:col + LANES] * dt_exp[:, col:col + LANES]
            rhs = jnp.concatenate(
                [jnp.where(low_half, xdt, 0.0), jnp.where(low_half, 0.0, xdt)], axis=0).astype(BF16)
            y_ref[:, col:col + LANES] = _dot(jnp.concatenate(ms, axis=1), rhs)
        gsl = slice(g * SSD_GROUP_WIDTH, (g + 1) * SSD_GROUP_WIDTH)
        for b in range(nb):
            rows = slice(b * seq, (b + 1) * seq)
            s_old = ssm_in_ref[b, gsl, :]
            yi = _dot_nt(cg[rows], s_old.astype(BF16))
            y_ref[rows, gsl] = y_ref[rows, gsl] + yi * e_exp[rows, gsl]
            xw = (xc_ref[rows, gsl] * w_exp[rows, gsl]).astype(BF16)
            upd = _dot_tn(xw, bg[rows])
            for r in range(SSD_HEADS_PER_GROUP):
                hh = g * SSD_HEADS_PER_GROUP + r
                hs = slice(r * SSD_HEAD_DIM, (r + 1) * SSD_HEAD_DIM)
                r0 = g * SSD_GROUP_WIDTH + r * SSD_HEAD_DIM
                ssm_ref[b, r0:r0 + SSD_HEAD_DIM, :] = s_old[hs] * dec[b, :, hh:hh + 1] + upd[hs]


def _gated_group_norm(proj_ref, xc_ref, y_ref, mix_ref, dskip_ref, normg_ref):
    z = proj_ref[:, COL_Z:COL_Z + SSD_WIDTH]
    y = (y_ref[...] + dskip_ref[...] * xc_ref[:, 0:SSD_WIDTH]) * _silu(z)
    for g in range(SSD_GROUPS):
        gsl = slice(g * SSD_GROUP_WIDTH, (g + 1) * SSD_GROUP_WIDTH)
        mix_ref[:, POOL_WIDTH + g * SSD_GROUP_WIDTH:POOL_WIDTH + (g + 1) * SSD_GROUP_WIDTH] = (
            _rms_scale(y[:, gsl], normg_ref[:, gsl])).astype(BF16)


def _softmax_rows(s):
    s = s - jnp.max(s, axis=-1, keepdims=True)
    p = jnp.exp(s)
    return p / jnp.sum(p, axis=-1, keepdims=True)


def _xattn_dense(proj_ref, mix_ref, k_ref, v_ref):
    inv_sqrt_d = 1.0 / math.sqrt(XA_HEAD_DIM)
    for hh in range(XA_HEADS):
        sl = slice(hh * XA_HEAD_DIM, (hh + 1) * XA_HEAD_DIM)
        q = proj_ref[:, COL_Q + hh * XA_HEAD_DIM:COL_Q + (hh + 1) * XA_HEAD_DIM].astype(BF16)
        p = _softmax_rows(_dot_nt(q, k_ref[:, sl]) * inv_sqrt_d)
        o = _dot(p.astype(BF16), v_ref[:, sl])
        gx = proj_ref[:, COL_GX + hh * XA_HEAD_DIM:COL_GX + (hh + 1) * XA_HEAD_DIM]
        m0 = POOL_WIDTH + SSD_WIDTH + hh * XA_HEAD_DIM
        mix_ref[:, m0:m0 + XA_HEAD_DIM] = (o * _silu(gx)).astype(BF16)


def _xattn_rowmajor(proj_ref, mix_ref, k_ref, v_ref, nb, seq):
    t = nb * seq
    inv_sqrt_d = 1.0 / math.sqrt(XA_HEAD_DIM)
    q4 = jnp.concatenate(
        [proj_ref[:, COL_Q + hh * XA_HEAD_DIM:COL_Q + (hh + 1) * XA_HEAD_DIM].astype(BF16).reshape(
            nb, seq, XA_HEAD_DIM) for hh in range(XA_HEADS)], axis=1)
    kf = k_ref[...].astype(BF16)
    vf = v_ref[...].astype(BF16)
    s = lax.dot_general(q4, kf, (((2,), (2,)), ((0,), (0,))), preferred_element_type=F32) * inv_sqrt_d
    shape = (nb, XA_HEADS * seq, N_MEM * XA_HEADS)
    row = lax.broadcasted_iota(jnp.int32, shape, 1)
    col = lax.broadcasted_iota(jnp.int32, shape, 2)
    own_head = jnp.bitwise_and(col, XA_HEADS - 1) * seq
    valid = jnp.bitwise_or(row - own_head, own_head + (seq - 1) - row) >= 0
    p = _softmax_rows(jnp.where(valid, s, -1e30))
    o4 = lax.dot_general(p.astype(BF16), vf, (((2,), (1,)), ((0,), (0,))), preferred_element_type=F32)
    for hh in range(XA_HEADS):
        o = o4[:, hh * seq:(hh + 1) * seq, :].reshape(t, XA_HEAD_DIM)
        gx = proj_ref[:, COL_GX + hh * XA_HEAD_DIM:COL_GX + (hh + 1) * XA_HEAD_DIM]
        m0 = POOL_WIDTH + SSD_WIDTH + hh * XA_HEAD_DIM
        mix_ref[:, m0:m0 + XA_HEAD_DIM] = (o * _silu(gx)).astype(BF16)


def _mem_kv_kernel(mem_ref, g_ref, w_ref, kf_ref, vf_ref, kd_ref, vd_ref):
    xn = _rms_scale(mem_ref[...], g_ref[...]).astype(BF16)
    kv = _dot(xn, w_ref[...])
    kd_ref[...] = kv[:, 0:XA_WIDTH].astype(BF16)
    vd_ref[...] = kv[:, XA_WIDTH:2 * XA_WIDTH].astype(BF16)
    for hh in range(XA_HEADS):
        kf_ref[pl.ds(hh, N_MEM, stride=XA_HEADS), :] = kv[:, hh * XA_HEAD_DIM:(hh + 1) * XA_HEAD_DIM]
        vf_ref[pl.ds(hh, N_MEM, stride=XA_HEADS), :] = kv[
            :, XA_WIDTH + hh * XA_HEAD_DIM:XA_WIDTH + (hh + 1) * XA_HEAD_DIM]


def _mem_kv(mem2d, mem_norm_g, w_kv):
    batch = mem2d.shape[0] // N_MEM
    flat = jax.ShapeDtypeStruct((DEPTH, batch, N_MEM * XA_HEADS, XA_HEAD_DIM), F32)
    dense = jax.ShapeDtypeStruct((DEPTH, batch, N_MEM, XA_WIDTH), BF16)
    flat_spec = pl.BlockSpec((None, None, N_MEM * XA_HEADS, XA_HEAD_DIM), lambda l, b: (l, b, 0, 0))
    dense_spec = pl.BlockSpec((None, None, N_MEM, XA_WIDTH), lambda l, b: (l, b, 0, 0))
    return pl.pallas_call(
        _mem_kv_kernel,
        grid=(DEPTH, batch),
        in_specs=[
            pl.BlockSpec((N_MEM, D_MODEL), lambda l, b: (b, 0)),
            pl.BlockSpec((None, 1, D_MODEL), lambda l, b: (l, 0, 0)),
            pl.BlockSpec((None, D_MODEL, 2 * XA_WIDTH), lambda l, b: (l, 0, 0)),
        ],
        out_specs=[flat_spec, flat_spec, dense_spec, dense_spec],
        out_shape=[flat, flat, dense, dense],
        compiler_params=pltpu.CompilerParams(dimension_semantics=("arbitrary", "arbitrary")),
        name="mem_kv",
    )(mem2d, mem_norm_g.reshape(DEPTH, 1, D_MODEL), w_kv)


def _prompt_kernel(x_ref, k_ref, v_ref, ng_ref, win_ref, wout_ref, poolw_ref, pscale_ref, convw_ref,
                   convb_ref, dtb_ref, arow_ref, dskip_ref, normg_ref, fng_ref, sel_ref,
                   y_out_ref, poolo_ref, convo_ref, ssm_ref,
                   h_ref, proj_ref, extp_ref, pa_ref, pb_ref, extc_ref, xc_ref, y_ref, mix_ref):
    layer = pl.program_id(1)
    c = pl.program_id(2)
    rows = pl.ds(pl.multiple_of(c * PROMPT_STEP, PROMPT_STEP), PROMPT_STEP)

    @pl.when(layer == 0)
    def _load():
        h_ref[rows, :] = x_ref[...]

    @pl.when(c == 0)
    def _init():
        _init_history(extp_ref, extc_ref, ssm_ref, None)

    h = h_ref[rows, :]
    proj_ref[...] = _dot(_rms_scale(h, ng_ref[...]).astype(BF16), win_ref[...])

    _pool_heads(proj_ref, extp_ref, pa_ref, pb_ref, mix_ref, poolw_ref, pscale_ref, 1, PROMPT_STEP,
                c * PROMPT_STEP)
    poolo_ref[...] = extp_ref[:, PROMPT_STEP + POOL_HIST - POOL_STATE_LEN:PROMPT_STEP + POOL_HIST, :]
    extp_ref[:, 0:POOL_HIST, :] = extp_ref[:, PROMPT_STEP:PROMPT_STEP + POOL_HIST, :]

    _conv_silu(proj_ref, extc_ref, xc_ref, convw_ref, convb_ref, 1, PROMPT_STEP)
    convo_ref[...] = extc_ref[:, PROMPT_STEP + CONV_HIST - (CONV_WIDTH - 1):PROMPT_STEP + CONV_HIST, :]
    extc_ref[:, CONV_HIST - (CONV_WIDTH - 1):CONV_HIST, :] = extc_ref[
        :, PROMPT_STEP + CONV_HIST - (CONV_WIDTH - 1):PROMPT_STEP + CONV_HIST, :]

    for s in range(PROMPT_STEP // CHUNK):
        sub = pl.ds(s * CHUNK, CHUNK)
        _ssd_chunk(proj_ref.at[sub], xc_ref.at[sub], y_ref.at[sub], ssm_ref, ssm_ref, dtb_ref,
                   arow_ref, sel_ref, 1, CHUNK)
    _gated_group_norm(proj_ref, xc_ref, y_ref, mix_ref, dskip_ref, normg_ref)
    _xattn_dense(proj_ref, mix_ref, k_ref, v_ref)

    h_new = h + _dot(mix_ref[...], wout_ref[...])
    h_ref[rows, :] = h_new

    @pl.when(layer == DEPTH - 1)
    def _final():
        y_out_ref[...] = _rms_scale(h_new, fng_ref[...])


def _prompt_stack(x2d, k_dense, v_dense, ng, w_in_r, lw, fng, sel, batch, seq_len):
    n_steps = seq_len // PROMPT_STEP
    assert n_steps * PROMPT_STEP == seq_len
    last = DEPTH - 1

    def lay3(shape):
        return pl.BlockSpec((None,) + shape, lambda b, l, c: (l, 0, 0))

    def per_seq(shape):
        return pl.BlockSpec((None, None) + shape, lambda b, l, c: (l, b, 0, 0))

    def state_out(shape):
        return pl.BlockSpec((None, 1) + shape, lambda b, l, c: (l, b, 0, 0))

    x_spec = pl.BlockSpec((PROMPT_STEP, D_MODEL),
                          lambda b, l, c: (b * n_steps + jnp.where(l == 0, c, n_steps - 1), 0))
    y_spec = pl.BlockSpec((PROMPT_STEP, D_MODEL),
                          lambda b, l, c: (b * n_steps + jnp.where(l == last, c, 0), 0))
    in_specs = [
        x_spec,
        per_seq((N_MEM, XA_WIDTH)), per_seq((N_MEM, XA_WIDTH)),
        lay3((1, D_MODEL)),
        lay3((D_MODEL, PROJ_WIDTH)),
        lay3((D_MIX, D_MODEL)),
        pl.BlockSpec((None, 4, POOL_GROUP_DIM, POOL_GROUP_DIM), lambda b, l, c: (l, 0, 0, 0)),
        lay3((1, POOL_WIDTH)),
        lay3((CONV_WIDTH, CONV_DIM)),
        lay3((1, CONV_DIM)),
        lay3((1, LANES)),
        lay3((1, LANES)),
        lay3((1, SSD_WIDTH)),
        lay3((1, SSD_WIDTH)),
        pl.BlockSpec((1, D_MODEL), lambda b, l, c: (0, 0)),
        pl.BlockSpec((LANES, N_EXPAND * SSD_WIDTH), lambda b, l, c: (0, 0)),
    ]
    out_specs = [
        y_spec,
        state_out((POOL_STATE_LEN, POOL_WIDTH)),
        state_out((CONV_WIDTH - 1, CONV_DIM)),
        state_out((SSD_WIDTH, D_STATE)),
    ]
    out_shape = [
        jax.ShapeDtypeStruct((batch * seq_len, D_MODEL), F32),
        jax.ShapeDtypeStruct((DEPTH, batch, POOL_STATE_LEN, POOL_WIDTH), F32),
        jax.ShapeDtypeStruct((DEPTH, batch, CONV_WIDTH - 1, CONV_DIM), F32),
        jax.ShapeDtypeStruct((DEPTH, batch, SSD_WIDTH, D_STATE), F32),
    ]
    scratch = [
        pltpu.VMEM((seq_len, D_MODEL), F32),
        pltpu.VMEM((PROMPT_STEP, PROJ_WIDTH), F32),
        pltpu.VMEM((1, POOL_HIST + PROMPT_STEP, POOL_WIDTH), F32),
        pltpu.VMEM((1, POOL_HIST + PROMPT_STEP, POOL_WIDTH), F32),
        pltpu.VMEM((1, POOL_HIST + PROMPT_STEP, POOL_WIDTH), F32),
        pltpu.VMEM((1, CONV_HIST + PROMPT_STEP, CONV_DIM), F32),
        pltpu.VMEM((PROMPT_STEP, CONV_DIM), F32),
        pltpu.VMEM((PROMPT_STEP, SSD_WIDTH), F32),
        pltpu.VMEM((PROMPT_STEP, D_MIX), BF16),
    ]
    return pl.pallas_call(
        _prompt_kernel,
        grid=(batch, DEPTH, n_steps),
        in_specs=in_specs,
        out_specs=out_specs,
        out_shape=out_shape,
        scratch_shapes=scratch,
        compiler_params=pltpu.CompilerParams(
            dimension_semantics=("arbitrary", "arbitrary", "arbitrary"),
            vmem_limit_bytes=VMEM_LIMIT_BYTES),
        name="prompt_stack",
    )(x2d, k_dense, v_dense, ng, w_in_r, *lw, fng, sel)


def _sample_kernel(*refs, nb, seq, n_alias, final):
    (h_ref, k_ref, v_ref, pool0_ref, conv0_ref, ssm0_ref, ng_ref, win_ref,
     wout_ref, poolw_ref, pscale_ref, convw_ref, convb_ref, dtb_ref, arow_ref, dskip_ref,
     normg_ref, fng_ref, sel_ref) = refs[:19]
    (hout_ref, poolo_ref, convo_ref, ssm_ref,
     proj_ref, extp_ref, pa_ref, pb_ref, extc_ref, xc_ref, y_ref, mix_ref) = refs[19 + n_alias:]

    proj_ref[...] = _dot(_rms_scale(h_ref[...], ng_ref[...]).astype(BF16), win_ref[...])
    _init_history(extp_ref, extc_ref, ssm_ref, (pool0_ref, conv0_ref))
    _pool_heads(proj_ref, extp_ref, pa_ref, pb_ref, mix_ref, poolw_ref, pscale_ref, nb, seq, PAST_LEN)
    poolo_ref[...] = extp_ref[:, seq + POOL_HIST - POOL_STATE_LEN:seq + POOL_HIST, :]
    _conv_silu(proj_ref, extc_ref, xc_ref, convw_ref, convb_ref, nb, seq)
    convo_ref[...] = extc_ref[:, seq + CONV_HIST - (CONV_WIDTH - 1):seq + CONV_HIST, :]
    _ssd_chunk(proj_ref, xc_ref, y_ref, ssm0_ref, ssm_ref, dtb_ref, arow_ref, sel_ref, nb, seq)
    _gated_group_norm(proj_ref, xc_ref, y_ref, mix_ref, dskip_ref, normg_ref)
    _xattn_rowmajor(proj_ref, mix_ref, k_ref, v_ref, nb, seq)
    h_new = h_ref[...] + _dot(mix_ref[...], wout_ref[...])
    hout_ref[...] = _rms_scale(h_new, fng_ref[...]) if final else h_new


def _sample_mixer(h, k_all, v_all, st_pool, st_conv, st_ssm, ng, w_in_r, lw, fng, sel, stacked, layer,
                  nb, seq):
    n_seq = h.shape[0] // seq
    t = nb * seq
    assert n_seq % nb == 0
    final = layer == DEPTH - 1

    def per_seq(shape):
        return pl.BlockSpec((None, nb) + shape, lambda i: (layer, i) + (0,) * len(shape))

    def lay(shape):
        return pl.BlockSpec((None,) + shape, lambda i: (layer,) + (0,) * len(shape))

    def rows_by_seq(n_rows, width):
        return pl.BlockSpec((None, n_rows, nb, width), lambda i: (layer, 0, i, 0))

    def resident(shape):
        return pl.BlockSpec((None,) + shape, lambda i: (layer,) + (0,) * len(shape),
                            pipeline_mode=pl.Buffered(1))

    rows = lambda width: pl.BlockSpec((t, width), lambda i: (i, 0))
    in_specs = [
        rows(D_MODEL),
        per_seq((N_MEM * XA_HEADS, XA_HEAD_DIM)), per_seq((N_MEM * XA_HEADS, XA_HEAD_DIM)),
        rows_by_seq(POOL_STATE_LEN, POOL_WIDTH), rows_by_seq(CONV_WIDTH - 1, CONV_DIM),
        per_seq((SSD_WIDTH, D_STATE)),
        lay((1, D_MODEL)), resident((D_MODEL, PROJ_WIDTH)),
        resident((D_MIX, D_MODEL)), lay((4, POOL_GROUP_DIM, POOL_GROUP_DIM)), lay((1, POOL_WIDTH)),
        lay((CONV_WIDTH, CONV_DIM)), lay((1, CONV_DIM)), lay((1, LANES)), lay((1, LANES)),
        lay((1, SSD_WIDTH)), lay((1, SSD_WIDTH)),
        pl.BlockSpec((1, D_MODEL), lambda i: (0, 0)),
        pl.BlockSpec((LANES, N_EXPAND * SSD_WIDTH), lambda i: (0, 0)),
    ]
    args = [h, k_all, v_all, st_pool, st_conv, st_ssm, ng, w_in_r, *lw, fng, sel]
    aliases = {}
    n_alias = 0
    if stacked is not None:
        n_alias = len(stacked)
        for j, arr in enumerate(stacked):
            aliases[len(args)] = 1 + j
            in_specs.append(pl.BlockSpec(memory_space=pl.ANY))
            args.append(arr)
    out_specs = [
        rows(D_MODEL),
        per_seq((POOL_STATE_LEN, POOL_WIDTH)), per_seq((CONV_WIDTH - 1, CONV_DIM)),
        per_seq((SSD_WIDTH, D_STATE)),
    ]
    out_shape = [
        jax.ShapeDtypeStruct(h.shape, F32),
        jax.ShapeDtypeStruct((DEPTH, n_seq, POOL_STATE_LEN, POOL_WIDTH), F32),
        jax.ShapeDtypeStruct((DEPTH, n_seq, CONV_WIDTH - 1, CONV_DIM), F32),
        jax.ShapeDtypeStruct((DEPTH, n_seq, SSD_WIDTH, D_STATE), F32),
    ]
    scratch = [
        pltpu.VMEM((t, PROJ_WIDTH), F32),
        pltpu.VMEM((nb, POOL_HIST + seq, POOL_WIDTH), F32),
        pltpu.VMEM((nb, POOL_HIST + seq, POOL_WIDTH), F32),
        pltpu.VMEM((nb, POOL_HIST + seq, POOL_WIDTH), F32),
        pltpu.VMEM((nb, CONV_HIST + seq, CONV_DIM), F32),
        pltpu.VMEM((t, CONV_DIM), F32),
        pltpu.VMEM((t, SSD_WIDTH), F32),
        pltpu.VMEM((t, D_MIX), BF16),
    ]
    return pl.pallas_call(
        functools.partial(_sample_kernel, nb=nb, seq=seq, n_alias=n_alias, final=final),
        grid=(n_seq // nb,),
        in_specs=in_specs,
        out_specs=out_specs,
        out_shape=out_shape,
        scratch_shapes=scratch,
        input_output_aliases=aliases,
        compiler_params=pltpu.CompilerParams(
            dimension_semantics=("arbitrary",), vmem_limit_bytes=VMEM_LIMIT_BYTES),
        name="sample_mixer",
    )(*args)


REORDER_BLOCK = 256
N_ALIGNED_BLOCKS = COL_Q // REORDER_BLOCK
N_MAIN_BLOCKS = COL_DT // REORDER_BLOCK


def _reorder_w_in_kernel(wt_ref, o_ref):
    j = pl.program_id(1)
    t = wt_ref[0].T.astype(BF16)

    @pl.when(j < N_MAIN_BLOCKS)
    def _main():
        o_ref[...] = t

    @pl.when(j == N_MAIN_BLOCKS)
    def _dt():
        o_ref[:, 0:LANES] = jnp.concatenate([t[:, 0:SSD_HEADS]] * DT_COPIES, axis=-1)
        o_ref[:, LANES:REORDER_BLOCK] = jnp.zeros((D_MODEL, REORDER_BLOCK - LANES), BF16)


def _reorder_w_in(w_in):
    depth, d, n = w_in.shape
    src_dt = COL_Q
    wt = jnp.swapaxes(w_in, 1, 2)

    def src_row(l, j):
        shifted = REORDER_BLOCK * j + jnp.where(j >= N_ALIGNED_BLOCKS, SSD_HEADS, 0)
        return (l, pl.multiple_of(jnp.where(j == N_MAIN_BLOCKS, src_dt, shifted), SSD_HEADS), 0)

    return pl.pallas_call(
        _reorder_w_in_kernel,
        grid=(depth, N_MAIN_BLOCKS + 1),
        in_specs=[pl.BlockSpec((pl.Element(1), pl.Element(REORDER_BLOCK), pl.Element(d)), src_row)],
        out_specs=pl.BlockSpec((None, d, REORDER_BLOCK), lambda l, j: (l, 0, j)),
        out_shape=jax.ShapeDtypeStruct((depth, d, PROJ_WIDTH), BF16),
        compiler_params=pltpu.CompilerParams(dimension_semantics=("arbitrary", "arbitrary")),
        name="reorder_w_in",
    )(wt)


def _expand_selector():
    lane = jnp.arange(LANES, dtype=jnp.int32)
    quantity = jnp.asarray(EXPAND_GROUPS, jnp.int32)[lane // SSD_HEADS]
    head = lane % SSD_HEADS
    col = jnp.arange(N_EXPAND * SSD_WIDTH, dtype=jnp.int32)
    col_quantity = col // SSD_WIDTH
    col_head = (col % SSD_WIDTH) // SSD_HEAD_DIM
    return jnp.logical_and(quantity[:, None] == col_quantity[None, :],
                           head[:, None] == col_head[None, :]).astype(BF16)


def kernel(x_prompt, x_sample, mem_prompt, state_pool, state_conv, state_ssm, cache_mem_k, cache_mem_v,
           norm_g, w_in, pool_w, pool_scale, conv_w, conv_b, dt_bias, a_log, d_skip, ssd_norm_g,
           mem_norm_g, w_mem_k, w_mem_v, w_out, final_norm_g):
    batch, seq_p, _ = x_prompt.shape
    dec_batch, seq_s, _ = x_sample.shape

    w_in_r = _reorder_w_in(w_in)
    w_kv = jnp.concatenate([w_mem_k, w_mem_v], axis=-1).astype(BF16)
    sel = _expand_selector()
    fng = final_norm_g.reshape(1, D_MODEL)
    ng = norm_g.reshape(DEPTH, 1, D_MODEL)
    lw = (w_out.astype(BF16), pool_w.astype(BF16), pool_scale.reshape(DEPTH, 1, POOL_WIDTH), conv_w,
          conv_b.reshape(DEPTH, 1, CONV_DIM),
          jnp.tile(dt_bias, (1, DT_COPIES)).reshape(DEPTH, 1, LANES),
          jnp.tile(-jnp.exp(a_log), (1, DT_COPIES)).reshape(DEPTH, 1, LANES),
          jnp.repeat(d_skip, SSD_HEAD_DIM, axis=-1).reshape(DEPTH, 1, SSD_WIDTH),
          ssd_norm_g.reshape(DEPTH, 1, SSD_WIDTH))

    k_flat, v_flat, k_dense, v_dense = _mem_kv(
        mem_prompt.reshape(batch * N_MEM, D_MODEL), mem_norm_g, w_kv)
    y_p, pool_p, conv_p, ssm_p = _prompt_stack(
        x_prompt.reshape(batch * seq_p, D_MODEL), k_dense, v_dense, ng, w_in_r, lw, fng, sel,
        batch, seq_p)

    hs = x_sample.reshape(dec_batch * seq_s, D_MODEL)
    st_ssm = state_ssm.reshape(DEPTH, dec_batch, SSD_WIDTH, D_STATE)
    st_pool = jnp.swapaxes(state_pool, 1, 2)
    st_conv = jnp.swapaxes(state_conv, 1, 2)
    ck = cache_mem_k.reshape(DEPTH, dec_batch, N_MEM * XA_HEADS, XA_HEAD_DIM)
    cv = cache_mem_v.reshape(DEPTH, dec_batch, N_MEM * XA_HEADS, XA_HEAD_DIM)
    stacked = None
    for layer in range(DEPTH):
        hs, *stacked = _sample_mixer(hs, ck, cv, st_pool, st_conv, st_ssm, ng, w_in_r, lw, fng, sel,
                                     stacked, layer, 8, seq_s)
    pool_s, conv_s, ssm_s = stacked

    mem_shape = (DEPTH, batch, N_MEM, XA_HEADS, XA_HEAD_DIM)
    return (y_p.reshape(batch, seq_p, D_MODEL), hs.reshape(dec_batch, seq_s, D_MODEL),
            pool_p, conv_p, ssm_p.reshape(DEPTH, batch, SSD_HEADS, SSD_HEAD_DIM, D_STATE),
            k_flat.reshape(mem_shape), v_flat.reshape(mem_shape),
            pool_s, conv_s, ssm_s.reshape(DEPTH, dec_batch, SSD_HEADS, SSD_HEAD_DIM, D_STATE))
```

```python
import functools
import math

import jax
import jax.numpy as jnp
from jax import lax
from jax.experimental import pallas as pl
from jax.experimental.pallas import tpu as pltpu

F32 = jnp.float32
BF16 = jnp.bfloat16

D_MODEL = 1024
DEPTH = 4
PAST_LEN = 16384
POOL_WIDTH = 512
POOL_WINDOWS = (2, 4, 8, 16)
POOL_GROUP_DIM = 128
POOL_STATE_LEN = 15
SSD_WIDTH = 1024
SSD_HEAD_DIM = 64
SSD_HEADS = 16
SSD_GROUPS = 2
SSD_HEADS_PER_GROUP = 8
SSD_GROUP_WIDTH = SSD_WIDTH // SSD_GROUPS
D_STATE = 128
CONV_WIDTH = 4
CONV_DIM = 1536
XA_WIDTH = 512
XA_HEADS = 4
XA_HEAD_DIM = 128
N_MEM = 256
D_MIX = 2048
EPS = 1e-6
LOG2E = 1.4426950408889634

LANES = 128
SUBLANES = 8
POOL_HIST = 32
CONV_HIST = 8
CHUNK = 128
PROMPT_STEP = 256

COL_U = 0
COL_GP = COL_U + POOL_WIDTH
COL_Z = COL_GP + POOL_WIDTH
COL_XBC = COL_Z + SSD_WIDTH
COL_Q = COL_XBC + CONV_DIM
COL_GX = COL_Q + XA_WIDTH
COL_DT = COL_GX + XA_WIDTH
DT_COPIES = LANES // SSD_HEADS
PROJ_WIDTH = COL_DT + 2 * LANES

EXPAND_GROUPS = (0, 0, 1, 1, 2, 2, 0, 1)
EXPAND_TERM = (0, 1, 0, 1, 0, 1, 2, 2)
N_EXPAND = 3

VMEM_LIMIT_BYTES = 60 * 1024 * 1024


def _silu(x):
    h = 0.5 * x
    return h + h * jnp.tanh(h)


def _softplus(x):
    return jnp.maximum(x, 0.0) + jnp.log1p(jnp.exp(-jnp.abs(x)))


def _bf16_terms(v, n):
    terms = []
    for _ in range(n):
        t = v.astype(BF16).astype(F32)
        terms.append(t)
        v = v - t
    return terms


def _dot(a, b):
    return jnp.dot(a, b, preferred_element_type=F32)


def _dot_nt(a, b):
    return lax.dot_general(a, b, (((1,), (1,)), ((), ())), preferred_element_type=F32)


def _dot_tn(a, b):
    return lax.dot_general(a, b, (((0,), (0,)), ((), ())), preferred_element_type=F32)


def _exact_left_dot(sel_bf16, v):
    return sum(_dot(sel_bf16, t.astype(BF16)) for t in _bf16_terms(v, 3))


def _rms_scale(x, g):
    return x * lax.rsqrt(jnp.mean(x * x, axis=-1, keepdims=True) + EPS) * g


def _init_history(extp_ref, extc_ref, ssm_ref, init_refs):
    nb = extp_ref.shape[0]
    extp_ref[:, 0:POOL_HIST, :] = jnp.zeros((nb, POOL_HIST, POOL_WIDTH), F32)
    extc_ref[:, 0:CONV_HIST, :] = jnp.zeros((nb, CONV_HIST, CONV_DIM), F32)
    if init_refs is None:
        ssm_ref[...] = jnp.zeros(ssm_ref.shape, F32)
    else:
        pool0_ref, conv0_ref = init_refs
        for r in range(POOL_STATE_LEN):
            extp_ref[:, POOL_HIST - POOL_STATE_LEN + r, :] = pool0_ref[r]
        for r in range(CONV_WIDTH - 1):
            extc_ref[:, CONV_HIST - (CONV_WIDTH - 1) + r, :] = conv0_ref[r]


def _pool_heads(proj_ref, extp_ref, pa_ref, pb_ref, mix_ref, poolw_ref, pscale_ref, nb, seq, pos0):
    t = nb * seq
    n = POOL_HIST + seq
    extp_ref[:, POOL_HIST:n, :] = proj_ref[:, COL_U:COL_U + POOL_WIDTH].reshape(nb, seq, POOL_WIDTH)
    g = POOL_GROUP_DIM
    pa_ref[:, 8:n, :] = extp_ref[:, 8:n, :] + extp_ref[:, 7:n - 1, :]
    pb_ref[:, 16:n, g:4 * g] = pa_ref[:, 16:n, g:4 * g] + pa_ref[:, 14:n - 2, g:4 * g]
    pa_ref[:, 24:n, 2 * g:4 * g] = pb_ref[:, 24:n, 2 * g:4 * g] + pb_ref[:, 20:n - 4, 2 * g:4 * g]
    pb_ref[:, 32:n, 3 * g:4 * g] = pa_ref[:, 32:n, 3 * g:4 * g] + pa_ref[:, 24:n - 8, 3 * g:4 * g]
    tok = lax.broadcasted_iota(jnp.int32, (nb, seq, LANES), 1)
    pos1 = pos0 + tok + 1
    for gi, w in enumerate(POOL_WINDOWS):
        sl = slice(gi * g, (gi + 1) * g)
        src = pa_ref if gi % 2 == 0 else pb_ref
        cnt = jnp.minimum(pos1, w).astype(F32)
        d = (src[:, POOL_HIST:n, sl] / cnt - extp_ref[:, POOL_HIST:n, sl]).reshape(t, g)
        o = _dot(d.astype(BF16), poolw_ref[gi]) * pscale_ref[:, sl]
        gp = proj_ref[:, COL_GP + gi * g:COL_GP + (gi + 1) * g]
        mix_ref[:, sl] = (o * _silu(gp)).astype(BF16)


def _conv_silu(proj_ref, extc_ref, xc_ref, convw_ref, convb_ref, nb, seq):
    t = nb * seq
    extc_ref[:, CONV_HIST:CONV_HIST + seq, :] = proj_ref[:, COL_XBC:COL_XBC + CONV_DIM].reshape(
        nb, seq, CONV_DIM)
    cblk = 256
    for j in range(CONV_DIM // cblk):
        sl = slice(j * cblk, (j + 1) * cblk)
        ext = extc_ref[:, :, sl]
        acc = convw_ref[0:1, sl][None] * ext
        for k in range(1, CONV_WIDTH):
            acc = pltpu.roll(acc, 1, 1) + convw_ref[k:k + 1, sl][None] * ext
        acc = acc[:, CONV_HIST:CONV_HIST + seq, :] + convb_ref[:, sl][None]
        xc_ref[:, sl] = _silu(acc).reshape(t, cblk)


def _expand_heads(vals, sel_ref):
    t = vals[0].shape[0]
    terms = [_bf16_terms(v, 1 + max(tm for gq, tm in zip(EXPAND_GROUPS, EXPAND_TERM) if gq == i))
             for i, v in enumerate(vals)]
    group = lax.broadcasted_iota(jnp.int32, (t, LANES), 1) // SSD_HEADS
    packed = terms[EXPAND_GROUPS[0]][EXPAND_TERM[0]]
    for gidx in range(1, len(EXPAND_GROUPS)):
        packed = jnp.where(group == gidx, terms[EXPAND_GROUPS[gidx]][EXPAND_TERM[gidx]], packed)
    out = _dot(packed.astype(BF16), sel_ref[...])
    return [out[:, i * SSD_WIDTH:(i + 1) * SSD_WIDTH] for i in range(len(vals))]


def _ssd_chunk(proj_ref, xc_ref, y_ref, ssm_in_ref, ssm_ref, dtb_ref, arow_ref, sel_ref, nb, seq):
    t = nb * seq
    dt = _softplus(proj_ref[:, COL_DT:COL_DT + LANES] + dtb_ref[...])
    a = dt * arow_ref[...]
    ri = lax.broadcasted_iota(jnp.int32, (t, t), 0)
    ci = lax.broadcasted_iota(jnp.int32, (t, t), 1)
    if nb > 1:
        seq_start = ri - lax.rem(ri, seq)
        causal = jnp.bitwise_or(ci - seq_start, ri - ci) >= 0
    else:
        causal = ci <= ri
    tri = jnp.where(causal, 1.0, 0.0).astype(BF16)
    cum = _exact_left_dot(tri, a)
    cum2 = cum * LOG2E
    pad = LANES - t
    if pad:
        cum2_t = jnp.concatenate([cum2, jnp.zeros((pad, LANES), F32)], axis=0).T[:, 0:t]
    else:
        cum2_t = cum2.T
    cum3 = cum.reshape(nb, seq, LANES)
    cum_last = cum3[:, seq - 1:seq, :]
    e_tok = jnp.exp(cum)
    w_tok = (jnp.exp(cum_last - cum3) * dt.reshape(nb, seq, LANES)).reshape(t, LANES)
    dec = jnp.exp(cum_last)
    e_exp, w_exp, dt_exp = _expand_heads([e_tok, w_tok, dt], sel_ref)

    lane = lax.broadcasted_iota(jnp.int32, (t, LANES), 1)
    low_half = lane < SSD_HEAD_DIM

    for g in range(SSD_GROUPS):
        b0 = SSD_WIDTH + g * D_STATE
        c0 = SSD_WIDTH + SSD_GROUPS * D_STATE + g * D_STATE
        bg = xc_ref[:, b0:b0 + D_STATE].astype(BF16)
        cg = xc_ref[:, c0:c0 + D_STATE].astype(BF16)
        cb = jnp.where(causal, _dot_nt(cg, bg), 0.0)
        for pp in range(SSD_HEADS_PER_GROUP // 2):
            h0 = g * SSD_HEADS_PER_GROUP + 2 * pp
            ms = []
            for hh in (h0, h0 + 1):
                seg2 = cum2[:, hh:hh + 1] - cum2_t[hh:hh + 1, :]
                ms.append((cb * jnp.exp2(jnp.minimum(seg2, 0.0))).astype(BF16))
            col = (h0 // 2) * LANES
            xdt = xc_ref[:, col:col + LANES] * dt_exp[:, col:col + LANES]
            rhs = jnp.concatenate(
                [jnp.where(low_half, xdt, 0.0), jnp.where(low_half, 0.0, xdt)], axis=0).astype(BF16)
            y_ref[:, col:col + LANES] = _dot(jnp.concatenate(ms, axis=1), rhs)
        gsl = slice(g * SSD_GROUP_WIDTH, (g + 1) * SSD_GROUP_WIDTH)
        for b in range(nb):
            rows = slice(b * seq, (b + 1) * seq)
            s_old = ssm_in_ref[b, gsl, :]
            yi = _dot_nt(cg[rows], s_old.astype(BF16))
            y_ref[rows, gsl] = y_ref[rows, gsl] + yi * e_exp[rows, gsl]
            xw = (xc_ref[rows, gsl] * w_exp[rows, gsl]).astype(BF16)
            upd = _dot_tn(xw, bg[rows])
            for r in range(SSD_HEADS_PER_GROUP):
                hh = g * SSD_HEADS_PER_GROUP + r
                hs = slice(r * SSD_HEAD_DIM, (r + 1) * SSD_HEAD_DIM)
                r0 = g * SSD_GROUP_WIDTH + r * SSD_HEAD_DIM
                ssm_ref[b, r0:r0 + SSD_HEAD_DIM, :] = s_old[hs] * dec[b, :, hh:hh + 1] + upd[hs]


def _gated_group_norm(proj_ref, xc_ref, y_ref, mix_ref, dskip_ref, normg_ref):
    z = proj_ref[:, COL_Z:COL_Z + SSD_WIDTH]
    y = (y_ref[...] + dskip_ref[...] * xc_ref[:, 0:SSD_WIDTH]) * _silu(z)
    for g in range(SSD_GROUPS):
        gsl = slice(g * SSD_GROUP_WIDTH, (g + 1) * SSD_GROUP_WIDTH)
        mix_ref[:, POOL_WIDTH + g * SSD_GROUP_WIDTH:POOL_WIDTH + (g + 1) * SSD_GROUP_WIDTH] = (
            _rms_scale(y[:, gsl], normg_ref[:, gsl])).astype(BF16)


def _softmax_rows(s):
    s = s - jnp.max(s, axis=-1, keepdims=True)
    p = jnp.exp(s)
    return p / jnp.sum(p, axis=-1, keepdims=True)


def _xattn_dense(proj_ref, mix_ref, k_ref, v_ref):
    inv_sqrt_d = 1.0 / math.sqrt(XA_HEAD_DIM)
    for hh in range(XA_HEADS):
        sl = slice(hh * XA_HEAD_DIM, (hh + 1) * XA_HEAD_DIM)
        q = proj_ref[:, COL_Q + hh * XA_HEAD_DIM:COL_Q + (hh + 1) * XA_HEAD_DIM].astype(BF16)
        p = _softmax_rows(_dot_nt(q, k_ref[:, sl]) * inv_sqrt_d)
        o = _dot(p.astype(BF16), v_ref[:, sl])
        gx = proj_ref[:, COL_GX + hh * XA_HEAD_DIM:COL_GX + (hh + 1) * XA_HEAD_DIM]
        m0 = POOL_WIDTH + SSD_WIDTH + hh * XA_HEAD_DIM
        mix_ref[:, m0:m0 + XA_HEAD_DIM] = (o * _silu(gx)).astype(BF16)


def _xattn_rowmajor(proj_ref, mix_ref, k_ref, v_ref, nb, seq):
    t = nb * seq
    inv_sqrt_d = 1.0 / math.sqrt(XA_HEAD_DIM)
    q4 = jnp.concatenate(
        [proj_ref[:, COL_Q + hh * XA_HEAD_DIM:COL_Q + (hh + 1) * XA_HEAD_DIM].astype(BF16).reshape(
            nb, seq, XA_HEAD_DIM) for hh in range(XA_HEADS)], axis=1)
    kf = k_ref[...].astype(BF16)
    vf = v_ref[...].astype(BF16)
    s = lax.dot_general(q4, kf, (((2,), (2,)), ((0,), (0,))), preferred_element_type=F32) * inv_sqrt_d
    shape = (nb, XA_HEADS * seq, N_MEM * XA_HEADS)
    row = lax.broadcasted_iota(jnp.int32, shape, 1)
    col = lax.broadcasted_iota(jnp.int32, shape, 2)
    own_head = jnp.bitwise_and(col, XA_HEADS - 1) * seq
    valid = jnp.bitwise_or(row - own_head, own_head + (seq - 1) - row) >= 0
    p = _softmax_rows(jnp.where(valid, s, -1e30))
    o4 = lax.dot_general(p.astype(BF16), vf, (((2,), (1,)), ((0,), (0,))), preferred_element_type=F32)
    for hh in range(XA_HEADS):
        o = o4[:, hh * seq:(hh + 1) * seq, :].reshape(t, XA_HEAD_DIM)
        gx = proj_ref[:, COL_GX + hh * XA_HEAD_DIM:COL_GX + (hh + 1) * XA_HEAD_DIM]
        m0 = POOL_WIDTH + SSD_WIDTH + hh * XA_HEAD_DIM
        mix_ref[:, m0:m0 + XA_HEAD_DIM] = (o * _silu(gx)).astype(BF16)


def _mem_kv_kernel(mem_ref, g_ref, w_ref, kf_ref, vf_ref, kd_ref, vd_ref):
    mem = mem_ref[...]
    inv = lax.rsqrt(jnp.mean(mem * mem, axis=-1, keepdims=True) + EPS)
    for l in range(DEPTH):
        xn = (mem * inv * g_ref[l]).astype(BF16)
        kv = _dot(xn, w_ref[l])
        kd_ref[l] = kv[:, 0:XA_WIDTH].astype(BF16)
        vd_ref[l] = kv[:, XA_WIDTH:2 * XA_WIDTH].astype(BF16)
        for hh in range(XA_HEADS):
            kf_ref[l, pl.ds(hh, N_MEM, stride=XA_HEADS), :] = kv[:, hh * XA_HEAD_DIM:(hh + 1) * XA_HEAD_DIM]
            vf_ref[l, pl.ds(hh, N_MEM, stride=XA_HEADS), :] = kv[
                :, XA_WIDTH + hh * XA_HEAD_DIM:XA_WIDTH + (hh + 1) * XA_HEAD_DIM]


def _mem_kv(mem2d, mem_norm_g, w_kv):
    batch = mem2d.shape[0] // N_MEM
    flat = jax.ShapeDtypeStruct((DEPTH, batch, N_MEM * XA_HEADS, XA_HEAD_DIM), F32)
    dense = jax.ShapeDtypeStruct((DEPTH, batch, N_MEM, XA_WIDTH), BF16)
    flat_spec = pl.BlockSpec((DEPTH, None, N_MEM * XA_HEADS, XA_HEAD_DIM), lambda b: (0, b, 0, 0))
    dense_spec = pl.BlockSpec((DEPTH, None, N_MEM, XA_WIDTH), lambda b: (0, b, 0, 0))
    return pl.pallas_call(
        _mem_kv_kernel,
        grid=(batch,),
        in_specs=[
            pl.BlockSpec((N_MEM, D_MODEL), lambda b: (b, 0)),
            pl.BlockSpec((DEPTH, 1, D_MODEL), lambda b: (0, 0, 0)),
            pl.BlockSpec((DEPTH, D_MODEL, 2 * XA_WIDTH), lambda b: (0, 0, 0)),
        ],
        out_specs=[flat_spec, flat_spec, dense_spec, dense_spec],
        out_shape=[flat, flat, dense, dense],
        compiler_params=pltpu.CompilerParams(
            dimension_semantics=("arbitrary",), vmem_limit_bytes=VMEM_LIMIT_BYTES),
        name="mem_kv",
    )(mem2d, mem_norm_g.reshape(DEPTH, 1, D_MODEL), w_kv)


def _prompt_kernel(x_ref, k_ref, v_ref, ng_ref, win_ref, wout_ref, poolw_ref, pscale_ref, convw_ref,
                   convb_ref, dtb_ref, arow_ref, dskip_ref, normg_ref, fng_ref, sel_ref,
                   y_out_ref, poolo_ref, convo_ref, ssm_ref,
                   h_ref, proj_ref, extp_ref, pa_ref, pb_ref, extc_ref, xc_ref, y_ref, mix_ref):
    layer = pl.program_id(1)
    c = pl.program_id(2)
    rows = pl.ds(pl.multiple_of(c * PROMPT_STEP, PROMPT_STEP), PROMPT_STEP)

    @pl.when(layer == 0)
    def _load():
        h_ref[rows, :] = x_ref[...]

    @pl.when(c == 0)
    def _init():
        _init_history(extp_ref, extc_ref, ssm_ref, None)

    h = h_ref[rows, :]
    proj_ref[...] = _dot(_rms_scale(h, ng_ref[...]).astype(BF16), win_ref[...])

    _pool_heads(proj_ref, extp_ref, pa_ref, pb_ref, mix_ref, poolw_ref, pscale_ref, 1, PROMPT_STEP,
                c * PROMPT_STEP)
    poolo_ref[...] = extp_ref[:, PROMPT_STEP + POOL_HIST - POOL_STATE_LEN:PROMPT_STEP + POOL_HIST, :]
    extp_ref[:, 0:POOL_HIST, :] = extp_ref[:, PROMPT_STEP:PROMPT_STEP + POOL_HIST, :]

    _conv_silu(proj_ref, extc_ref, xc_ref, convw_ref, convb_ref, 1, PROMPT_STEP)
    convo_ref[...] = extc_ref[:, PROMPT_STEP + CONV_HIST - (CONV_WIDTH - 1):PROMPT_STEP + CONV_HIST, :]
    extc_ref[:, CONV_HIST - (CONV_WIDTH - 1):CONV_HIST, :] = extc_ref[
        :, PROMPT_STEP + CONV_HIST - (CONV_WIDTH - 1):PROMPT_STEP + CONV_HIST, :]

    for s in range(PROMPT_STEP // CHUNK):
        sub = pl.ds(s * CHUNK, CHUNK)
        _ssd_chunk(proj_ref.at[sub], xc_ref.at[sub], y_ref.at[sub], ssm_ref, ssm_ref, dtb_ref,
                   arow_ref, sel_ref, 1, CHUNK)
    _gated_group_norm(proj_ref, xc_ref, y_ref, mix_ref, dskip_ref, normg_ref)
    _xattn_dense(proj_ref, mix_ref, k_ref, v_ref)

    h_new = h + _dot(mix_ref[...], wout_ref[...])
    h_ref[rows, :] = h_new

    @pl.when(layer == DEPTH - 1)
    def _final():
        y_out_ref[...] = _rms_scale(h_new, fng_ref[...])


def _prompt_stack(x2d, k_dense, v_dense, ng, w_in_r, lw, fng, sel, batch, seq_len):
    n_steps = seq_len // PROMPT_STEP
    assert n_steps * PROMPT_STEP == seq_len
    last = DEPTH - 1

    def lay3(shape):
        return pl.BlockSpec((None,) + shape, lambda b, l, c: (l, 0, 0))

    def per_seq(shape):
        return pl.BlockSpec((None, None) + shape, lambda b, l, c: (l, b, 0, 0))

    def state_out(shape):
        return pl.BlockSpec((None, 1) + shape, lambda b, l, c: (l, b, 0, 0))

    x_spec = pl.BlockSpec((PROMPT_STEP, D_MODEL),
                          lambda b, l, c: (b * n_steps + jnp.where(l == 0, c, n_steps - 1), 0))
    y_spec = pl.BlockSpec((PROMPT_STEP, D_MODEL),
                          lambda b, l, c: (b * n_steps + jnp.where(l == last, c, 0), 0))
    in_specs = [
        x_spec,
        per_seq((N_MEM, XA_WIDTH)), per_seq((N_MEM, XA_WIDTH)),
        lay3((1, D_MODEL)),
        lay3((D_MODEL, PROJ_WIDTH)),
        lay3((D_MIX, D_MODEL)),
        pl.BlockSpec((None, 4, POOL_GROUP_DIM, POOL_GROUP_DIM), lambda b, l, c: (l, 0, 0, 0)),
        lay3((1, POOL_WIDTH)),
        lay3((CONV_WIDTH, CONV_DIM)),
        lay3((1, CONV_DIM)),
        lay3((1, LANES)),
        lay3((1, LANES)),
        lay3((1, SSD_WIDTH)),
        lay3((1, SSD_WIDTH)),
        pl.BlockSpec((1, D_MODEL), lambda b, l, c: (0, 0)),
        pl.BlockSpec((LANES, N_EXPAND * SSD_WIDTH), lambda b, l, c: (0, 0)),
    ]
    out_specs = [
        y_spec,
        state_out((POOL_STATE_LEN, POOL_WIDTH)),
        state_out((CONV_WIDTH - 1, CONV_DIM)),
        state_out((SSD_WIDTH, D_STATE)),
    ]
    out_shape = [
        jax.ShapeDtypeStruct((batch * seq_len, D_MODEL), F32),
        jax.ShapeDtypeStruct((DEPTH, batch, POOL_STATE_LEN, POOL_WIDTH), F32),
        jax.ShapeDtypeStruct((DEPTH, batch, CONV_WIDTH - 1, CONV_DIM), F32),
        jax.ShapeDtypeStruct((DEPTH, batch, SSD_WIDTH, D_STATE), F32),
    ]
    scratch = [
        pltpu.VMEM((seq_len, D_MODEL), F32),
        pltpu.VMEM((PROMPT_STEP, PROJ_WIDTH), F32),
        pltpu.VMEM((1, POOL_HIST + PROMPT_STEP, POOL_WIDTH), F32),
        pltpu.VMEM((1, POOL_HIST + PROMPT_STEP, POOL_WIDTH), F32),
        pltpu.VMEM((1, POOL_HIST + PROMPT_STEP, POOL_WIDTH), F32),
        pltpu.VMEM((1, CONV_HIST + PROMPT_STEP, CONV_DIM), F32),
        pltpu.VMEM((PROMPT_STEP, CONV_DIM), F32),
        pltpu.VMEM((PROMPT_STEP, SSD_WIDTH), F32),
        pltpu.VMEM((PROMPT_STEP, D_MIX), BF16),
    ]
    return pl.pallas_call(
        _prompt_kernel,
        grid=(batch, DEPTH, n_steps),
        in_specs=in_specs,
        out_specs=out_specs,
        out_shape=out_shape,
        scratch_shapes=scratch,
        compiler_params=pltpu.CompilerParams(
            dimension_semantics=("arbitrary", "arbitrary", "arbitrary"),
            vmem_limit_bytes=VMEM_LIMIT_BYTES),
        name="prompt_stack",
    )(x2d, k_dense, v_dense, ng, w_in_r, *lw, fng, sel)


def _sample_kernel(*refs, nb, seq, n_alias, final):
    (h_ref, k_ref, v_ref, pool0_ref, conv0_ref, ssm0_ref, ng_ref, win_ref,
     wout_ref, poolw_ref, pscale_ref, convw_ref, convb_ref, dtb_ref, arow_ref, dskip_ref,
     normg_ref, fng_ref, sel_ref) = refs[:19]
    (hout_ref, poolo_ref, convo_ref, ssm_ref,
     proj_ref, extp_ref, pa_ref, pb_ref, extc_ref, xc_ref, y_ref, mix_ref) = refs[19 + n_alias:]

    proj_ref[...] = _dot(_rms_scale(h_ref[...], ng_ref[...]).astype(BF16), win_ref[...])
    _init_history(extp_ref, extc_ref, ssm_ref, (pool0_ref, conv0_ref))
    _pool_heads(proj_ref, extp_ref, pa_ref, pb_ref, mix_ref, poolw_ref, pscale_ref, nb, seq, PAST_LEN)
    poolo_ref[...] = extp_ref[:, seq + POOL_HIST - POOL_STATE_LEN:seq + POOL_HIST, :]
    _conv_silu(proj_ref, extc_ref, xc_ref, convw_ref, convb_ref, nb, seq)
    convo_ref[...] = extc_ref[:, seq + CONV_HIST - (CONV_WIDTH - 1):seq + CONV_HIST, :]
    _ssd_chunk(proj_ref, xc_ref, y_ref, ssm0_ref, ssm_ref, dtb_ref, arow_ref, sel_ref, nb, seq)
    _gated_group_norm(proj_ref, xc_ref, y_ref, mix_ref, dskip_ref, normg_ref)
    _xattn_rowmajor(proj_ref, mix_ref, k_ref, v_ref, nb, seq)
    h_new = h_ref[...] + _dot(mix_ref[...], wout_ref[...])
    hout_ref[...] = _rms_scale(h_new, fng_ref[...]) if final else h_new


def _sample_mixer(h, k_all, v_all, st_pool, st_conv, st_ssm, ng, w_in_r, lw, fng, sel, stacked, layer,
                  nb, seq):
    n_seq = h.shape[0] // seq
    t = nb * seq
    assert n_seq % nb == 0
    final = layer == DEPTH - 1

    def per_seq(shape):
        return pl.BlockSpec((None, nb) + shape, lambda i: (layer, i) + (0,) * len(shape))

    def lay(shape):
        return pl.BlockSpec((None,) + shape, lambda i: (layer,) + (0,) * len(shape))

    def rows_by_seq(n_rows, width):
        return pl.BlockSpec((None, n_rows, nb, width), lambda i: (layer, 0, i, 0))

    def resident(shape):
        return pl.BlockSpec((None,) + shape, lambda i: (layer,) + (0,) * len(shape),
                            pipeline_mode=pl.Buffered(1))

    rows = lambda width: pl.BlockSpec((t, width), lambda i: (i, 0))
    in_specs = [
        rows(D_MODEL),
        per_seq((N_MEM * XA_HEADS, XA_HEAD_DIM)), per_seq((N_MEM * XA_HEADS, XA_HEAD_DIM)),
        rows_by_seq(POOL_STATE_LEN, POOL_WIDTH), rows_by_seq(CONV_WIDTH - 1, CONV_DIM),
        per_seq((SSD_WIDTH, D_STATE)),
        lay((1, D_MODEL)), resident((D_MODEL, PROJ_WIDTH)),
        resident((D_MIX, D_MODEL)), lay((4, POOL_GROUP_DIM, POOL_GROUP_DIM)), lay((1, POOL_WIDTH)),
        lay((CONV_WIDTH, CONV_DIM)), lay((1, CONV_DIM)), lay((1, LANES)), lay((1, LANES)),
        lay((1, SSD_WIDTH)), lay((1, SSD_WIDTH)),
        pl.BlockSpec((1, D_MODEL), lambda i: (0, 0)),
        pl.BlockSpec((LANES, N_EXPAND * SSD_WIDTH), lambda i: (0, 0)),
    ]
    args = [h, k_all, v_all, st_pool, st_conv, st_ssm, ng, w_in_r, *lw, fng, sel]
    aliases = {}
    n_alias = 0
    if stacked is not None:
        n_alias = len(stacked)
        for j, arr in enumerate(stacked):
            aliases[len(args)] = 1 + j
            in_specs.append(pl.BlockSpec(memory_space=pl.ANY))
            args.append(arr)
    out_specs = [
        rows(D_MODEL),
        per_seq((POOL_STATE_LEN, POOL_WIDTH)), per_seq((CONV_WIDTH - 1, CONV_DIM)),
        per_seq((SSD_WIDTH, D_STATE)),
    ]
    out_shape = [
        jax.ShapeDtypeStruct(h.shape, F32),
        jax.ShapeDtypeStruct((DEPTH, n_seq, POOL_STATE_LEN, POOL_WIDTH), F32),
        jax.ShapeDtypeStruct((DEPTH, n_seq, CONV_WIDTH - 1, CONV_DIM), F32),
        jax.ShapeDtypeStruct((DEPTH, n_seq, SSD_WIDTH, D_STATE), F32),
    ]
    scratch = [
        pltpu.VMEM((t, PROJ_WIDTH), F32),
        pltpu.VMEM((nb, POOL_HIST + seq, POOL_WIDTH), F32),
        pltpu.VMEM((nb, POOL_HIST + seq, POOL_WIDTH), F32),
        pltpu.VMEM((nb, POOL_HIST + seq, POOL_WIDTH), F32),
        pltpu.VMEM((nb, CONV_HIST + seq, CONV_DIM), F32),
        pltpu.VMEM((t, CONV_DIM), F32),
        pltpu.VMEM((t, SSD_WIDTH), F32),
        pltpu.VMEM((t, D_MIX), BF16),
    ]
    return pl.pallas_call(
        functools.partial(_sample_kernel, nb=nb, seq=seq, n_alias=n_alias, final=final),
        grid=(n_seq // nb,),
        in_specs=in_specs,
        out_specs=out_specs,
        out_shape=out_shape,
        scratch_shapes=scratch,
        input_output_aliases=aliases,
        compiler_params=pltpu.CompilerParams(
            dimension_semantics=("arbitrary",), vmem_limit_bytes=VMEM_LIMIT_BYTES),
        name="sample_mixer",
    )(*args)


REORDER_BLOCK = 256
N_ALIGNED_BLOCKS = COL_Q // REORDER_BLOCK
N_MAIN_BLOCKS = COL_DT // REORDER_BLOCK


def _reorder_w_in_kernel(wt_ref, o_ref):
    j = pl.program_id(1)
    t = wt_ref[0].T.astype(BF16)

    @pl.when(j < N_MAIN_BLOCKS)
    def _main():
        o_ref[...] = t

    @pl.when(j == N_MAIN_BLOCKS)
    def _dt():
        o_ref[:, 0:LANES] = jnp.concatenate([t[:, 0:SSD_HEADS]] * DT_COPIES, axis=-1)
        o_ref[:, LANES:REORDER_BLOCK] = jnp.zeros((D_MODEL, REORDER_BLOCK - LANES), BF16)


def _reorder_w_in(w_in):
    depth, d, n = w_in.shape
    src_dt = COL_Q
    wt = jnp.swapaxes(w_in, 1, 2)

    def src_row(l, j):
        shifted = REORDER_BLOCK * j + jnp.where(j >= N_ALIGNED_BLOCKS, SSD_HEADS, 0)
        return (l, pl.multiple_of(jnp.where(j == N_MAIN_BLOCKS, src_dt, shifted), SSD_HEADS), 0)

    return pl.pallas_call(
        _reorder_w_in_kernel,
        grid=(depth, N_MAIN_BLOCKS + 1),
        in_specs=[pl.BlockSpec((pl.Element(1), pl.Element(REORDER_BLOCK), pl.Element(d)), src_row)],
        out_specs=pl.BlockSpec((None, d, REORDER_BLOCK), lambda l, j: (l, 0, j)),
        out_shape=jax.ShapeDtypeStruct((depth, d, PROJ_WIDTH), BF16),
        compiler_params=pltpu.CompilerParams(dimension_semantics=("arbitrary", "arbitrary")),
        name="reorder_w_in",
    )(wt)


def _expand_selector():
    lane = jnp.arange(LANES, dtype=jnp.int32)
    quantity = jnp.asarray(EXPAND_GROUPS, jnp.int32)[lane // SSD_HEADS]
    head = lane % SSD_HEADS
    col = jnp.arange(N_EXPAND * SSD_WIDTH, dtype=jnp.int32)
    col_quantity = col // SSD_WIDTH
    col_head = (col % SSD_WIDTH) // SSD_HEAD_DIM
    return jnp.logical_and(quantity[:, None] == col_quantity[None, :],
                           head[:, None] == col_head[None, :]).astype(BF16)


def kernel(x_prompt, x_sample, mem_prompt, state_pool, state_conv, state_ssm, cache_mem_k, cache_mem_v,
           norm_g, w_in, pool_w, pool_scale, conv_w, conv_b, dt_bias, a_log, d_skip, ssd_norm_g,
           mem_norm_g, w_mem_k, w_mem_v, w_out, final_norm_g):
    batch, seq_p, _ = x_prompt.shape
    dec_batch, seq_s, _ = x_sample.shape

    w_in_r = _reorder_w_in(w_in)
    w_kv = jnp.concatenate([w_mem_k, w_mem_v], axis=-1).astype(BF16)
    sel = _expand_selector()
    fng = final_norm_g.reshape(1, D_MODEL)
    ng = norm_g.reshape(DEPTH, 1, D_MODEL)
    lw = (w_out.astype(BF16), pool_w.astype(BF16), pool_scale.reshape(DEPTH, 1, POOL_WIDTH), conv_w,
          conv_b.reshape(DEPTH, 1, CONV_DIM),
          jnp.tile(dt_bias, (1, DT_COPIES)).reshape(DEPTH, 1, LANES),
          jnp.tile(-jnp.exp(a_log), (1, DT_COPIES)).reshape(DEPTH, 1, LANES),
          jnp.repeat(d_skip, SSD_HEAD_DIM, axis=-1).reshape(DEPTH, 1, SSD_WIDTH),
          ssd_norm_g.reshape(DEPTH, 1, SSD_WIDTH))

    k_flat, v_flat, k_dense, v_dense = _mem_kv(
        mem_prompt.reshape(batch * N_MEM, D_MODEL), mem_norm_g, w_kv)
    y_p, pool_p, conv_p, ssm_p = _prompt_stack(
        x_prompt.reshape(batch * seq_p, D_MODEL), k_dense, v_dense, ng, w_in_r, lw, fng, sel,
        batch, seq_p)

    hs = x_sample.reshape(dec_batch * seq_s, D_MODEL)
    st_ssm = state_ssm.reshape(DEPTH, dec_batch, SSD_WIDTH, D_STATE)
    st_pool = jnp.swapaxes(state_pool, 1, 2)
    st_conv = jnp.swapaxes(state_conv, 1, 2)
    ck = cache_mem_k.reshape(DEPTH, dec_batch, N_MEM * XA_HEADS, XA_HEAD_DIM)
    cv = cache_mem_v.reshape(DEPTH, dec_batch, N_MEM * XA_HEADS, XA_HEAD_DIM)
    stacked = None
    for layer in range(DEPTH):
        hs, *stacked = _sample_mixer(hs, ck, cv, st_pool, st_conv, st_ssm, ng, w_in_r, lw, fng, sel,
                                     stacked, layer, 8, seq_s)
    pool_s, conv_s, ssm_s = stacked

    mem_shape = (DEPTH, batch, N_MEM, XA_HEADS, XA_HEAD_DIM)
    return (y_p.reshape(batch, seq_p, D_MODEL), hs.reshape(dec_batch, seq_s, D_MODEL),
            pool_p, conv_p, ssm_p.reshape(DEPTH, batch, SSD_HEADS, SSD_HEAD_DIM, D_STATE),
            k_flat.reshape(mem_shape), v_flat.reshape(mem_shape),
            pool_s, conv_s, ssm_s.reshape(DEPTH, dec_batch, SSD_HEADS, SSD_HEAD_DIM, D_STATE))
```

```python
import functools
import math

import jax
import jax.numpy as jnp
from jax import lax
from jax.experimental import pallas as pl
from jax.experimental.pallas import tpu as pltpu

F32 = jnp.float32
BF16 = jnp.bfloat16

D_MODEL = 1024
DEPTH = 4
PAST_LEN = 16384
POOL_WIDTH = 512
POOL_WINDOWS = (2, 4, 8, 16)
POOL_GROUP_DIM = 128
POOL_STATE_LEN = 15
SSD_WIDTH = 1024
SSD_HEAD_DIM = 64
SSD_HEADS = 16
SSD_GROUPS = 2
SSD_HEADS_PER_GROUP = 8
SSD_GROUP_WIDTH = SSD_WIDTH // SSD_GROUPS
D_STATE = 128
CONV_WIDTH = 4
CONV_DIM = 1536
XA_WIDTH = 512
XA_HEADS = 4
XA_HEAD_DIM = 128
N_MEM = 256
D_MIX = 2048
EPS = 1e-6
LOG2E = 1.4426950408889634

LANES = 128
SUBLANES = 8
POOL_HIST = 32
CONV_HIST = 8
CHUNK = 128
PROMPT_STEP = 256

COL_U = 0
COL_GP = COL_U + POOL_WIDTH
COL_Z = COL_GP + POOL_WIDTH
COL_XBC = COL_Z + SSD_WIDTH
COL_Q = COL_XBC + CONV_DIM
COL_GX = COL_Q + XA_WIDTH
COL_DT = COL_GX + XA_WIDTH
DT_COPIES = LANES // SSD_HEADS
PROJ_WIDTH = COL_DT + 2 * LANES

EXPAND_GROUPS = (0, 0, 1, 1, 2, 2, 0, 1)
EXPAND_TERM = (0, 1, 0, 1, 0, 1, 2, 2)
N_EXPAND = 3

VMEM_LIMIT_BYTES = 60 * 1024 * 1024


def _silu(x):
    h = 0.5 * x
    return h + h * jnp.tanh(h)


def _softplus(x):
    return jnp.maximum(x, 0.0) + jnp.log1p(jnp.exp(-jnp.abs(x)))


def _bf16_terms(v, n):
    terms = []
    for _ in range(n):
        t = v.astype(BF16).astype(F32)
        terms.append(t)
        v = v - t
    return terms


def _dot(a, b):
    return jnp.dot(a, b, preferred_element_type=F32)


def _dot_nt(a, b):
    return lax.dot_general(a, b, (((1,), (1,)), ((), ())), preferred_element_type=F32)


def _dot_tn(a, b):
    return lax.dot_general(a, b, (((0,), (0,)), ((), ())), preferred_element_type=F32)


def _exact_left_dot(sel_bf16, v):
    return sum(_dot(sel_bf16, t.astype(BF16)) for t in _bf16_terms(v, 3))


def _rms_scale(x, g):
    return x * lax.rsqrt(jnp.mean(x * x, axis=-1, keepdims=True) + EPS) * g


def _init_history(extp_ref, extc_ref, ssm_ref, init_refs):
    nb = extp_ref.shape[0]
    extp_ref[:, 0:POOL_HIST, :] = jnp.zeros((nb, POOL_HIST, POOL_WIDTH), F32)
    extc_ref[:, 0:CONV_HIST, :] = jnp.zeros((nb, CONV_HIST, CONV_DIM), F32)
    if init_refs is None:
        ssm_ref[...] = jnp.zeros(ssm_ref.shape, F32)
    else:
        pool0_ref, conv0_ref = init_refs
        for r in range(POOL_STATE_LEN):
            extp_ref[:, POOL_HIST - POOL_STATE_LEN + r, :] = pool0_ref[r]
        for r in range(CONV_WIDTH - 1):
            extc_ref[:, CONV_HIST - (CONV_WIDTH - 1) + r, :] = conv0_ref[r]


def _pool_heads(proj_ref, extp_ref, pa_ref, pb_ref, mix_ref, poolw_ref, pscale_ref, nb, seq, pos0):
    t = nb * seq
    n = POOL_HIST + seq
    extp_ref[:, POOL_HIST:n, :] = proj_ref[:, COL_U:COL_U + POOL_WIDTH].reshape(nb, seq, POOL_WIDTH)
    g = POOL_GROUP_DIM
    pa_ref[:, 8:n, :] = extp_ref[:, 8:n, :] + extp_ref[:, 7:n - 1, :]
    pb_ref[:, 16:n, g:4 * g] = pa_ref[:, 16:n, g:4 * g] + pa_ref[:, 14:n - 2, g:4 * g]
    pa_ref[:, 24:n, 2 * g:4 * g] = pb_ref[:, 24:n, 2 * g:4 * g] + pb_ref[:, 20:n - 4, 2 * g:4 * g]
    pb_ref[:, 32:n, 3 * g:4 * g] = pa_ref[:, 32:n, 3 * g:4 * g] + pa_ref[:, 24:n - 8, 3 * g:4 * g]
    tok = lax.broadcasted_iota(jnp.int32, (nb, seq, LANES), 1)
    pos1 = pos0 + tok + 1
    for gi, w in enumerate(POOL_WINDOWS):
        sl = slice(gi * g, (gi + 1) * g)
        src = pa_ref if gi % 2 == 0 else pb_ref
        cnt = jnp.minimum(pos1, w).astype(F32)
        d = (src[:, POOL_HIST:n, sl] / cnt - extp_ref[:, POOL_HIST:n, sl]).reshape(t, g)
        o = _dot(d.astype(BF16), poolw_ref[gi]) * pscale_ref[:, sl]
        gp = proj_ref[:, COL_GP + gi * g:COL_GP + (gi + 1) * g]
        mix_ref[:, sl] = (o * _silu(gp)).astype(BF16)


def _conv_silu(proj_ref, extc_ref, xc_ref, convw_ref, convb_ref, nb, seq):
    t = nb * seq
    extc_ref[:, CONV_HIST:CONV_HIST + seq, :] = proj_ref[:, COL_XBC:COL_XBC + CONV_DIM].reshape(
        nb, seq, CONV_DIM)
    cblk = 256
    for j in range(CONV_DIM // cblk):
        sl = slice(j * cblk, (j + 1) * cblk)
        ext = extc_ref[:, :, sl]
        acc = convw_ref[0:1, sl][None] * ext
        for k in range(1, CONV_WIDTH):
            acc = pltpu.roll(acc, 1, 1) + convw_ref[k:k + 1, sl][None] * ext
        acc = acc[:, CONV_HIST:CONV_HIST + seq, :] + convb_ref[:, sl][None]
        xc_ref[:, sl] = _silu(acc).reshape(t, cblk)


def _expand_heads(vals, sel_ref):
    t = vals[0].shape[0]
    terms = [_bf16_terms(v, 1 + max(tm for gq, tm in zip(EXPAND_GROUPS, EXPAND_TERM) if gq == i))
             for i, v in enumerate(vals)]
    group = lax.broadcasted_iota(jnp.int32, (t, LANES), 1) // SSD_HEADS
    packed = terms[EXPAND_GROUPS[0]][EXPAND_TERM[0]]
    for gidx in range(1, len(EXPAND_GROUPS)):
        packed = jnp.where(group == gidx, terms[EXPAND_GROUPS[gidx]][EXPAND_TERM[gidx]], packed)
    out = _dot(packed.astype(BF16), sel_ref[...])
    return [out[:, i * SSD_WIDTH:(i + 1) * SSD_WIDTH] for i in range(len(vals))]


def _ssd_chunk(proj_ref, xc_ref, y_ref, ssm_in_ref, ssm_ref, dtb_ref, arow_ref, sel_ref, nb, seq):
    t = nb * seq
    dt = _softplus(proj_ref[:, COL_DT:COL_DT + LANES] + dtb_ref[...])
    a = dt * arow_ref[...]
    ri = lax.broadcasted_iota(jnp.int32, (t, t), 0)
    ci = lax.broadcasted_iota(jnp.int32, (t, t), 1)
    if nb > 1:
        seq_start = ri - lax.rem(ri, seq)
        causal = jnp.bitwise_or(ci - seq_start, ri - ci) >= 0
    else:
        causal = ci <= ri
    tri = jnp.where(causal, 1.0, 0.0).astype(BF16)
    cum = _exact_left_dot(tri, a)
    cum2 = cum * LOG2E
    pad = LANES - t
    if pad:
        cum2_t = jnp.concatenate([cum2, jnp.zeros((pad, LANES), F32)], axis=0).T[:, 0:t]
    else:
        cum2_t = cum2.T
    cum3 = cum.reshape(nb, seq, LANES)
    cum_last = cum3[:, seq - 1:seq, :]
    e_tok = jnp.exp(cum)
    w_tok = (jnp.exp(cum_last - cum3) * dt.reshape(nb, seq, LANES)).reshape(t, LANES)
    dec = jnp.exp(cum_last)
    e_exp, w_exp, dt_exp = _expand_heads([e_tok, w_tok, dt], sel_ref)

    lane = lax.broadcasted_iota(jnp.int32, (t, LANES), 1)
    low_half = lane < SSD_HEAD_DIM

    for g in range(SSD_GROUPS):
        b0 = SSD_WIDTH + g * D_STATE
        c0 = SSD_WIDTH + SSD_GROUPS * D_STATE + g * D_STATE
        bg = xc_ref[:, b0:b0 + D_STATE].astype(BF16)
        cg = xc_ref[:, c0:c0 + D_STATE].astype(BF16)
        cb = jnp.where(causal, _dot_nt(cg, bg), 0.0)
        for pp in range(SSD_HEADS_PER_GROUP // 2):
            h0 = g * SSD_HEADS_PER_GROUP + 2 * pp
            ms = []
            for hh in (h0, h0 + 1):
                seg2 = cum2[:, hh:hh + 1] - cum2_t[hh:hh + 1, :]
                ms.append((cb * jnp.exp2(jnp.minimum(seg2, 0.0))).astype(BF16))
            col = (h0 // 2) * LANES
            xdt = xc_ref[:, col:col + LANES] * dt_exp[:, col:col + LANES]
            rhs = jnp.concatenate(
                [jnp.where(low_half, xdt, 0.0), jnp.where(low_half, 0.0, xdt)], axis=0).astype(BF16)
            y_ref[:, col:col + LANES] = _dot(jnp.concatenate(ms, axis=1), rhs)
        gsl = slice(g * SSD_GROUP_WIDTH, (g + 1) * SSD_GROUP_WIDTH)
        for b in range(nb):
            rows = slice(b * seq, (b + 1) * seq)
            s_old = ssm_in_ref[b, gsl, :]
            yi = _dot_nt(cg[rows], s_old.astype(BF16))
            y_ref[rows, gsl] = y_ref[rows, gsl] + yi * e_exp[rows, gsl]
            xw = (xc_ref[rows, gsl] * w_exp[rows, gsl]).astype(BF16)
            upd = _dot_tn(xw, bg[rows])
            for r in range(SSD_HEADS_PER_GROUP):
                hh = g * SSD_HEADS_PER_GROUP + r
                hs = slice(r * SSD_HEAD_DIM, (r + 1) * SSD_HEAD_DIM)
                r0 = g * SSD_GROUP_WIDTH + r * SSD_HEAD_DIM
                ssm_ref[b, r0:r0 + SSD_HEAD_DIM, :] = s_old[hs] * dec[b, :, hh:hh + 1] + upd[hs]


def _gated_group_norm(proj_ref, xc_ref, y_ref, mix_ref, dskip_ref, normg_ref):
    z = proj_ref[:, COL_Z:COL_Z + SSD_WIDTH]
    y = (y_ref[...] + dskip_ref[...] * xc_ref[:, 0:SSD_WIDTH]) * _silu(z)
    for g in range(SSD_GROUPS):
        gsl = slice(g * SSD_GROUP_WIDTH, (g + 1) * SSD_GROUP_WIDTH)
        mix_ref[:, POOL_WIDTH + g * SSD_GROUP_WIDTH:POOL_WIDTH + (g + 1) * SSD_GROUP_WIDTH] = (
            _rms_scale(y[:, gsl], normg_ref[:, gsl])).astype(BF16)


def _softmax_rows(s):
    s = s - jnp.max(s, axis=-1, keepdims=True)
    p = jnp.exp(s)
    return p / jnp.sum(p, axis=-1, keepdims=True)


def _xattn_dense(proj_ref, mix_ref, k_ref, v_ref):
    inv_sqrt_d = 1.0 / math.sqrt(XA_HEAD_DIM)
    for hh in range(XA_HEADS):
        sl = slice(hh * XA_HEAD_DIM, (hh + 1) * XA_HEAD_DIM)
        q = proj_ref[:, COL_Q + hh * XA_HEAD_DIM:COL_Q + (hh + 1) * XA_HEAD_DIM].astype(BF16)
        p = _softmax_rows(_dot_nt(q, k_ref[:, sl]) * inv_sqrt_d)
        o = _dot(p.astype(BF16), v_ref[:, sl])
        gx = proj_ref[:, COL_GX + hh * XA_HEAD_DIM:COL_GX + (hh + 1) * XA_HEAD_DIM]
        m0 = POOL_WIDTH + SSD_WIDTH + hh * XA_HEAD_DIM
        mix_ref[:, m0:m0 + XA_HEAD_DIM] = (o * _silu(gx)).astype(BF16)


def _xattn_rowmajor(proj_ref, mix_ref, k_ref, v_ref, nb, seq):
    t = nb * seq
    inv_sqrt_d = 1.0 / math.sqrt(XA_HEAD_DIM)
    q4 = jnp.concatenate(
        [proj_ref[:, COL_Q + hh * XA_HEAD_DIM:COL_Q + (hh + 1) * XA_HEAD_DIM].astype(BF16).reshape(
            nb, seq, XA_HEAD_DIM) for hh in range(XA_HEADS)], axis=1)
    kf = k_ref[...].astype(BF16)
    vf = v_ref[...].astype(BF16)
    s = lax.dot_general(q4, kf, (((2,), (2,)), ((0,), (0,))), preferred_element_type=F32) * inv_sqrt_d
    shape = (nb, XA_HEADS * seq, N_MEM * XA_HEADS)
    row = lax.broadcasted_iota(jnp.int32, shape, 1)
    col = lax.broadcasted_iota(jnp.int32, shape, 2)
    own_head = jnp.bitwise_and(col, XA_HEADS - 1) * seq
    valid = jnp.bitwise_or(row - own_head, own_head + (seq - 1) - row) >= 0
    p = _softmax_rows(jnp.where(valid, s, -1e30))
    o4 = lax.dot_general(p.astype(BF16), vf, (((2,), (1,)), ((0,), (0,))), preferred_element_type=F32)
    for hh in range(XA_HEADS):
        o = o4[:, hh * seq:(hh + 1) * seq, :].reshape(t, XA_HEAD_DIM)
        gx = proj_ref[:, COL_GX + hh * XA_HEAD_DIM:COL_GX + (hh + 1) * XA_HEAD_DIM]
        m0 = POOL_WIDTH + SSD_WIDTH + hh * XA_HEAD_DIM
        mix_ref[:, m0:m0 + XA_HEAD_DIM] = (o * _silu(gx)).astype(BF16)


def _mem_kv_kernel(mem_ref, g_ref, wk_ref, wv_ref, kf_ref, vf_ref, kd_ref, vd_ref):
    mem = mem_ref[...]
    inv = lax.rsqrt(jnp.mean(mem * mem, axis=-1, keepdims=True) + EPS)
    for l in range(DEPTH):
        xn = (mem * inv * g_ref[l]).astype(BF16)
        for w_ref, flat_ref, dense_ref in ((wk_ref, kf_ref, kd_ref), (wv_ref, vf_ref, vd_ref)):
            out = _dot(xn, w_ref[l].astype(BF16))
            dense_ref[l] = out.astype(BF16)
            for hh in range(XA_HEADS):
                flat_ref[l, pl.ds(hh, N_MEM, stride=XA_HEADS), :] = out[
                    :, hh * XA_HEAD_DIM:(hh + 1) * XA_HEAD_DIM]


def _mem_kv(mem2d, mem_norm_g, w_k, w_v):
    batch = mem2d.shape[0] // N_MEM
    flat = jax.ShapeDtypeStruct((DEPTH, batch, N_MEM * XA_HEADS, XA_HEAD_DIM), F32)
    dense = jax.ShapeDtypeStruct((DEPTH, batch, N_MEM, XA_WIDTH), BF16)
    flat_spec = pl.BlockSpec((DEPTH, None, N_MEM * XA_HEADS, XA_HEAD_DIM), lambda b: (0, b, 0, 0))
    dense_spec = pl.BlockSpec((DEPTH, None, N_MEM, XA_WIDTH), lambda b: (0, b, 0, 0))
    return pl.pallas_call(
        _mem_kv_kernel,
        grid=(batch,),
        in_specs=[
            pl.BlockSpec((N_MEM, D_MODEL), lambda b: (b, 0)),
            pl.BlockSpec((DEPTH, 1, D_MODEL), lambda b: (0, 0, 0)),
            pl.BlockSpec((DEPTH, D_MODEL, XA_WIDTH), lambda b: (0, 0, 0), pipeline_mode=pl.Buffered(1)),
            pl.BlockSpec((DEPTH, D_MODEL, XA_WIDTH), lambda b: (0, 0, 0), pipeline_mode=pl.Buffered(1)),
        ],
        out_specs=[flat_spec, flat_spec, dense_spec, dense_spec],
        out_shape=[flat, flat, dense, dense],
        compiler_params=pltpu.CompilerParams(
            dimension_semantics=("arbitrary",), vmem_limit_bytes=VMEM_LIMIT_BYTES),
        name="mem_kv",
    )(mem2d, mem_norm_g.reshape(DEPTH, 1, D_MODEL), w_k, w_v)


def _prompt_kernel(x_ref, k_ref, v_ref, ng_ref, win_ref, wout_ref, poolw_ref, pscale_ref, convw_ref,
                   convb_ref, dtb_ref, arow_ref, dskip_ref, normg_ref, fng_ref, sel_ref,
                   y_out_ref, poolo_ref, convo_ref, ssm_ref,
                   h_ref, proj_ref, extp_ref, pa_ref, pb_ref, extc_ref, xc_ref, y_ref, mix_ref):
    layer = pl.program_id(1)
    c = pl.program_id(2)
    rows = pl.ds(pl.multiple_of(c * PROMPT_STEP, PROMPT_STEP), PROMPT_STEP)

    @pl.when(layer == 0)
    def _load():
        h_ref[rows, :] = x_ref[...]

    @pl.when(c == 0)
    def _init():
        _init_history(extp_ref, extc_ref, ssm_ref, None)

    h = h_ref[rows, :]
    proj_ref[...] = _dot(_rms_scale(h, ng_ref[...]).astype(BF16), win_ref[...])

    _pool_heads(proj_ref, extp_ref, pa_ref, pb_ref, mix_ref, poolw_ref, pscale_ref, 1, PROMPT_STEP,
                c * PROMPT_STEP)
    poolo_ref[...] = extp_ref[:, PROMPT_STEP + POOL_HIST - POOL_STATE_LEN:PROMPT_STEP + POOL_HIST, :]
    extp_ref[:, 0:POOL_HIST, :] = extp_ref[:, PROMPT_STEP:PROMPT_STEP + POOL_HIST, :]

    _conv_silu(proj_ref, extc_ref, xc_ref, convw_ref, convb_ref, 1, PROMPT_STEP)
    convo_ref[...] = extc_ref[:, PROMPT_STEP + CONV_HIST - (CONV_WIDTH - 1):PROMPT_STEP + CONV_HIST, :]
    extc_ref[:, CONV_HIST - (CONV_WIDTH - 1):CONV_HIST, :] = extc_ref[
        :, PROMPT_STEP + CONV_HIST - (CONV_WIDTH - 1):PROMPT_STEP + CONV_HIST, :]

    for s in range(PROMPT_STEP // CHUNK):
        sub = pl.ds(s * CHUNK, CHUNK)
        _ssd_chunk(proj_ref.at[sub], xc_ref.at[sub], y_ref.at[sub], ssm_ref, ssm_ref, dtb_ref,
                   arow_ref, sel_ref, 1, CHUNK)
    _gated_group_norm(proj_ref, xc_ref, y_ref, mix_ref, dskip_ref, normg_ref)
    _xattn_dense(proj_ref, mix_ref, k_ref, v_ref)

    h_new = h + _dot(mix_ref[...], wout_ref[...])
    h_ref[rows, :] = h_new

    @pl.when(layer == DEPTH - 1)
    def _final():
        y_out_ref[...] = _rms_scale(h_new, fng_ref[...])


def _prompt_stack(x2d, k_dense, v_dense, ng, w_in_r, lw, fng, sel, batch, seq_len):
    n_steps = seq_len // PROMPT_STEP
    assert n_steps * PROMPT_STEP == seq_len
    last = DEPTH - 1

    def lay3(shape):
        return pl.BlockSpec((None,) + shape, lambda b, l, c: (l, 0, 0))

    def per_seq(shape):
        return pl.BlockSpec((None, None) + shape, lambda b, l, c: (l, b, 0, 0))

    def state_out(shape):
        return pl.BlockSpec((None, 1) + shape, lambda b, l, c: (l, b, 0, 0))

    x_spec = pl.BlockSpec((PROMPT_STEP, D_MODEL),
                          lambda b, l, c: (b * n_steps + jnp.where(l == 0, c, n_steps - 1), 0))
    y_spec = pl.BlockSpec((PROMPT_STEP, D_MODEL),
                          lambda b, l, c: (b * n_steps + jnp.where(l == last, c, 0), 0))
    in_specs = [
        x_spec,
        per_seq((N_MEM, XA_WIDTH)), per_seq((N_MEM, XA_WIDTH)),
        lay3((1, D_MODEL)),
        lay3((D_MODEL, PROJ_WIDTH)),
        lay3((D_MIX, D_MODEL)),
        pl.BlockSpec((None, 4, POOL_GROUP_DIM, POOL_GROUP_DIM), lambda b, l, c: (l, 0, 0, 0)),
        lay3((1, POOL_WIDTH)),
        lay3((CONV_WIDTH, CONV_DIM)),
        lay3((1, CONV_DIM)),
        lay3((1, LANES)),
        lay3((1, LANES)),
        lay3((1, SSD_WIDTH)),
        lay3((1, SSD_WIDTH)),
        pl.BlockSpec((1, D_MODEL), lambda b, l, c: (0, 0)),
        pl.BlockSpec((LANES, N_EXPAND * SSD_WIDTH), lambda b, l, c: (0, 0)),
    ]
    out_specs = [
        y_spec,
        state_out((POOL_STATE_LEN, POOL_WIDTH)),
        state_out((CONV_WIDTH - 1, CONV_DIM)),
        state_out((SSD_WIDTH, D_STATE)),
    ]
    out_shape = [
        jax.ShapeDtypeStruct((batch * seq_len, D_MODEL), F32),
        jax.ShapeDtypeStruct((DEPTH, batch, POOL_STATE_LEN, POOL_WIDTH), F32),
        jax.ShapeDtypeStruct((DEPTH, batch, CONV_WIDTH - 1, CONV_DIM), F32),
        jax.ShapeDtypeStruct((DEPTH, batch, SSD_WIDTH, D_STATE), F32),
    ]
    scratch = [
        pltpu.VMEM((seq_len, D_MODEL), F32),
        pltpu.VMEM((PROMPT_STEP, PROJ_WIDTH), F32),
        pltpu.VMEM((1, POOL_HIST + PROMPT_STEP, POOL_WIDTH), F32),
        pltpu.VMEM((1, POOL_HIST + PROMPT_STEP, POOL_WIDTH), F32),
        pltpu.VMEM((1, POOL_HIST + PROMPT_STEP, POOL_WIDTH), F32),
        pltpu.VMEM((1, CONV_HIST + PROMPT_STEP, CONV_DIM), F32),
        pltpu.VMEM((PROMPT_STEP, CONV_DIM), F32),
        pltpu.VMEM((PROMPT_STEP, SSD_WIDTH), F32),
        pltpu.VMEM((PROMPT_STEP, D_MIX), BF16),
    ]
    return pl.pallas_call(
        _prompt_kernel,
        grid=(batch, DEPTH, n_steps),
        in_specs=in_specs,
        out_specs=out_specs,
        out_shape=out_shape,
        scratch_shapes=scratch,
        compiler_params=pltpu.CompilerParams(
            dimension_semantics=("arbitrary", "arbitrary", "arbitrary"),
            vmem_limit_bytes=VMEM_LIMIT_BYTES),
        name="prompt_stack",
    )(x2d, k_dense, v_dense, ng, w_in_r, *lw, fng, sel)


def _sample_kernel(*refs, nb, seq, n_alias, final):
    (h_ref, k_ref, v_ref, pool0_ref, conv0_ref, ssm0_ref, ng_ref, win_ref,
     wout_ref, poolw_ref, pscale_ref, convw_ref, convb_ref, dtb_ref, arow_ref, dskip_ref,
     normg_ref, fng_ref, sel_ref) = refs[:19]
    (hout_ref, poolo_ref, convo_ref, ssm_ref,
     proj_ref, extp_ref, pa_ref, pb_ref, extc_ref, xc_ref, y_ref, mix_ref) = refs[19 + n_alias:]

    proj_ref[...] = _dot(_rms_scale(h_ref[...], ng_ref[...]).astype(BF16), win_ref[...])
    _init_history(extp_ref, extc_ref, ssm_ref, (pool0_ref, conv0_ref))
    _pool_heads(proj_ref, extp_ref, pa_ref, pb_ref, mix_ref, poolw_ref, pscale_ref, nb, seq, PAST_LEN)
    poolo_ref[...] = extp_ref[:, seq + POOL_HIST - POOL_STATE_LEN:seq + POOL_HIST, :]
    _conv_silu(proj_ref, extc_ref, xc_ref, convw_ref, convb_ref, nb, seq)
    convo_ref[...] = extc_ref[:, seq + CONV_HIST - (CONV_WIDTH - 1):seq + CONV_HIST, :]
    _ssd_chunk(proj_ref, xc_ref, y_ref, ssm0_ref, ssm_ref, dtb_ref, arow_ref, sel_ref, nb, seq)
    _gated_group_norm(proj_ref, xc_ref, y_ref, mix_ref, dskip_ref, normg_ref)
    _xattn_rowmajor(proj_ref, mix_ref, k_ref, v_ref, nb, seq)
    h_new = h_ref[...] + _dot(mix_ref[...], wout_ref[...])
    hout_ref[...] = _rms_scale(h_new, fng_ref[...]) if final else h_new


def _sample_mixer(h, k_all, v_all, st_pool, st_conv, st_ssm, ng, w_in_r, lw, fng, sel, stacked, layer,
                  nb, seq):
    n_seq = h.shape[0] // seq
    t = nb * seq
    assert n_seq % nb == 0
    final = layer == DEPTH - 1

    def per_seq(shape):
        return pl.BlockSpec((None, nb) + shape, lambda i: (layer, i) + (0,) * len(shape))

    def lay(shape):
        return pl.BlockSpec((None,) + shape, lambda i: (layer,) + (0,) * len(shape))

    def rows_by_seq(n_rows, width):
        return pl.BlockSpec((None, n_rows, nb, width), lambda i: (layer, 0, i, 0))

    def resident(shape):
        return pl.BlockSpec((None,) + shape, lambda i: (layer,) + (0,) * len(shape),
                            pipeline_mode=pl.Buffered(1))

    rows = lambda width: pl.BlockSpec((t, width), lambda i: (i, 0))
    in_specs = [
        rows(D_MODEL),
        per_seq((N_MEM * XA_HEADS, XA_HEAD_DIM)), per_seq((N_MEM * XA_HEADS, XA_HEAD_DIM)),
        rows_by_seq(POOL_STATE_LEN, POOL_WIDTH), rows_by_seq(CONV_WIDTH - 1, CONV_DIM),
        per_seq((SSD_WIDTH, D_STATE)),
        lay((1, D_MODEL)), resident((D_MODEL, PROJ_WIDTH)),
        resident((D_MIX, D_MODEL)), lay((4, POOL_GROUP_DIM, POOL_GROUP_DIM)), lay((1, POOL_WIDTH)),
        lay((CONV_WIDTH, CONV_DIM)), lay((1, CONV_DIM)), lay((1, LANES)), lay((1, LANES)),
        lay((1, SSD_WIDTH)), lay((1, SSD_WIDTH)),
        pl.BlockSpec((1, D_MODEL), lambda i: (0, 0)),
        pl.BlockSpec((LANES, N_EXPAND * SSD_WIDTH), lambda i: (0, 0)),
    ]
    args = [h, k_all, v_all, st_pool, st_conv, st_ssm, ng, w_in_r, *lw, fng, sel]
    aliases = {}
    n_alias = 0
    if stacked is not None:
        n_alias = len(stacked)
        for j, arr in enumerate(stacked):
            aliases[len(args)] = 1 + j
            in_specs.append(pl.BlockSpec(memory_space=pl.ANY))
            args.append(arr)
    out_specs = [
        rows(D_MODEL),
        per_seq((POOL_STATE_LEN, POOL_WIDTH)), per_seq((CONV_WIDTH - 1, CONV_DIM)),
        per_seq((SSD_WIDTH, D_STATE)),
    ]
    out_shape = [
        jax.ShapeDtypeStruct(h.shape, F32),
        jax.ShapeDtypeStruct((DEPTH, n_seq, POOL_STATE_LEN, POOL_WIDTH), F32),
        jax.ShapeDtypeStruct((DEPTH, n_seq, CONV_WIDTH - 1, CONV_DIM), F32),
        jax.ShapeDtypeStruct((DEPTH, n_seq, SSD_WIDTH, D_STATE), F32),
    ]
    scratch = [
        pltpu.VMEM((t, PROJ_WIDTH), F32),
        pltpu.VMEM((nb, POOL_HIST + seq, POOL_WIDTH), F32),
        pltpu.VMEM((nb, POOL_HIST + seq, POOL_WIDTH), F32),
        pltpu.VMEM((nb, POOL_HIST + seq, POOL_WIDTH), F32),
        pltpu.VMEM((nb, CONV_HIST + seq, CONV_DIM), F32),
        pltpu.VMEM((t, CONV_DIM), F32),
        pltpu.VMEM((t, SSD_WIDTH), F32),
        pltpu.VMEM((t, D_MIX), BF16),
    ]
    return pl.pallas_call(
        functools.partial(_sample_kernel, nb=nb, seq=seq, n_alias=n_alias, final=final),
        grid=(n_seq // nb,),
        in_specs=in_specs,
        out_specs=out_specs,
        out_shape=out_shape,
        scratch_shapes=scratch,
        input_output_aliases=aliases,
        compiler_params=pltpu.CompilerParams(
            dimension_semantics=("arbitrary",), vmem_limit_bytes=VMEM_LIMIT_BYTES),
        name="sample_mixer",
    )(*args)


REORDER_BLOCK = 256
N_ALIGNED_BLOCKS = COL_Q // REORDER_BLOCK
N_MAIN_BLOCKS = COL_DT // REORDER_BLOCK


def _reorder_w_in_kernel(wt_ref, o_ref):
    j = pl.program_id(1)
    t = wt_ref[0].T.astype(BF16)

    @pl.when(j < N_MAIN_BLOCKS)
    def _main():
        o_ref[...] = t

    @pl.when(j == N_MAIN_BLOCKS)
    def _dt():
        o_ref[:, 0:LANES] = jnp.concatenate([t[:, 0:SSD_HEADS]] * DT_COPIES, axis=-1)
        o_ref[:, LANES:REORDER_BLOCK] = jnp.zeros((D_MODEL, REORDER_BLOCK - LANES), BF16)


def _reorder_w_in(w_in):
    depth, d, n = w_in.shape
    src_dt = COL_Q
    wt = jnp.swapaxes(w_in, 1, 2)

    def src_row(l, j):
        shifted = REORDER_BLOCK * j + jnp.where(j >= N_ALIGNED_BLOCKS, SSD_HEADS, 0)
        return (l, pl.multiple_of(jnp.where(j == N_MAIN_BLOCKS, src_dt, shifted), SSD_HEADS), 0)

    return pl.pallas_call(
        _reorder_w_in_kernel,
        grid=(depth, N_MAIN_BLOCKS + 1),
        in_specs=[pl.BlockSpec((pl.Element(1), pl.Element(REORDER_BLOCK), pl.Element(d)), src_row)],
        out_specs=pl.BlockSpec((None, d, REORDER_BLOCK), lambda l, j: (l, 0, j)),
        out_shape=jax.ShapeDtypeStruct((depth, d, PROJ_WIDTH), BF16),
        compiler_params=pltpu.CompilerParams(dimension_semantics=("arbitrary", "arbitrary")),
        name="reorder_w_in",
    )(wt)


def _expand_selector():
    lane = jnp.arange(LANES, dtype=jnp.int32)
    quantity = jnp.asarray(EXPAND_GROUPS, jnp.int32)[lane // SSD_HEADS]
    head = lane % SSD_HEADS
    col = jnp.arange(N_EXPAND * SSD_WIDTH, dtype=jnp.int32)
    col_quantity = col // SSD_WIDTH
    col_head = (col % SSD_WIDTH) // SSD_HEAD_DIM
    return jnp.logical_and(quantity[:, None] == col_quantity[None, :],
                           head[:, None] == col_head[None, :]).astype(BF16)


def kernel(x_prompt, x_sample, mem_prompt, state_pool, state_conv, state_ssm, cache_mem_k, cache_mem_v,
           norm_g, w_in, pool_w, pool_scale, conv_w, conv_b, dt_bias, a_log, d_skip, ssd_norm_g,
           mem_norm_g, w_mem_k, w_mem_v, w_out, final_norm_g):
    batch, seq_p, _ = x_prompt.shape
    dec_batch, seq_s, _ = x_sample.shape

    w_in_r = _reorder_w_in(w_in)
    sel = _expand_selector()
    fng = final_norm_g.reshape(1, D_MODEL)
    ng = norm_g.reshape(DEPTH, 1, D_MODEL)
    lw = (w_out.astype(BF16), pool_w.astype(BF16), pool_scale.reshape(DEPTH, 1, POOL_WIDTH), conv_w,
          conv_b.reshape(DEPTH, 1, CONV_DIM),
          jnp.tile(dt_bias, (1, DT_COPIES)).reshape(DEPTH, 1, LANES),
          jnp.tile(-jnp.exp(a_log), (1, DT_COPIES)).reshape(DEPTH, 1, LANES),
          jnp.repeat(d_skip, SSD_HEAD_DIM, axis=-1).reshape(DEPTH, 1, SSD_WIDTH),
          ssd_norm_g.reshape(DEPTH, 1, SSD_WIDTH))

    k_flat, v_flat, k_dense, v_dense = _mem_kv(
        mem_prompt.reshape(batch * N_MEM, D_MODEL), mem_norm_g, w_mem_k, w_mem_v)
    y_p, pool_p, conv_p, ssm_p = _prompt_stack(
        x_prompt.reshape(batch * seq_p, D_MODEL), k_dense, v_dense, ng, w_in_r, lw, fng, sel,
        batch, seq_p)

    hs = x_sample.reshape(dec_batch * seq_s, D_MODEL)
    st_ssm = state_ssm.reshape(DEPTH, dec_batch, SSD_WIDTH, D_STATE)
    st_pool = jnp.swapaxes(state_pool, 1, 2)
    st_conv = jnp.swapaxes(state_conv, 1, 2)
    ck = cache_mem_k.reshape(DEPTH, dec_batch, N_MEM * XA_HEADS, XA_HEAD_DIM)
    cv = cache_mem_v.reshape(DEPTH, dec_batch, N_MEM * XA_HEADS, XA_HEAD_DIM)
    stacked = None
    for layer in range(DEPTH):
        hs, *stacked = _sample_mixer(hs, ck, cv, st_pool, st_conv, st_ssm, ng, w_in_r, lw, fng, sel,
                                     stacked, layer, 8, seq_s)
    pool_s, conv_s, ssm_s = stacked

    mem_shape = (DEPTH, batch, N_MEM, XA_HEADS, XA_HEAD_DIM)
    return (y_p.reshape(batch, seq_p, D_MODEL), hs.reshape(dec_batch, seq_s, D_MODEL),
            pool_p, conv_p, ssm_p.reshape(DEPTH, batch, SSD_HEADS, SSD_HEAD_DIM, D_STATE),
            k_flat.reshape(mem_shape), v_flat.reshape(mem_shape),
            pool_s, conv_s, ssm_s.reshape(DEPTH, dec_batch, SSD_HEADS, SSD_HEAD_DIM, D_STATE))
```
